```python
import math
import jax, jax.numpy as jnp
from jax import lax
import numpy as np

D_MODEL = 1024
BATCH = 32
SEQ = 2048
DEPTH = 4

N_MIXERS = 2
D_FF = 4 * D_MODEL
EPS = 1e-6

DA_HEAD_DIM = 64
DA_V_DIM = 2 * DA_HEAD_DIM
DA_HEADS = D_MODEL // DA_V_DIM
DA_QK_WIDTH = DA_HEADS * 2 * DA_HEAD_DIM
DA_V_WIDTH = DA_HEADS * DA_V_DIM
Q_BLOCK = 128

GD_HEAD_DIM = 128
GD_HEADS = D_MODEL // GD_HEAD_DIM
GD_WIDTH = GD_HEADS * GD_HEAD_DIM
GD_CONV = 4
GD_CHUNK = 64
GD_IN_WIDTH = 4 * GD_WIDTH + 2 * GD_HEADS

kernel_name = "hybrid_diffattn_gdn_sqrelu"


def rms_norm(x, g):
    xf = x.astype(jnp.float32)
    y = xf * lax.rsqrt(jnp.mean(xf * xf, axis=-1, keepdims=True) + EPS)
    return (y * g.astype(jnp.float32)).astype(x.dtype)


def l2_norm(x):
    return x * lax.rsqrt(jnp.sum(x * x, axis=-1, keepdims=True) + EPS)


def lambda_init_fn(layer):
    return 0.8 - 0.6 * math.exp(-0.3 * layer)


def diff_attention(xn, w_in, q_norm, k_norm, lq1, lk1, lq2, lk2, sub_norm, w_out, layer):
    B, S, _ = xn.shape
    proj = xn @ w_in
    q = proj[..., :DA_QK_WIDTH]
    k = proj[..., DA_QK_WIDTH:2 * DA_QK_WIDTH]
    v = proj[..., 2 * DA_QK_WIDTH:]
    q = q.reshape(B, S, DA_HEADS, 2, DA_HEAD_DIM).transpose(0, 2, 3, 1, 4)
    k = k.reshape(B, S, DA_HEADS, 2, DA_HEAD_DIM).transpose(0, 2, 3, 1, 4)
    v = v.reshape(B, S, DA_HEADS, DA_V_DIM).transpose(0, 2, 1, 3)
    q = rms_norm(q, q_norm)
    k = rms_norm(k, k_norm)
    lam_init = lambda_init_fn(layer)
    lam = (jnp.exp(jnp.sum(lq1.astype(jnp.float32) * lk1.astype(jnp.float32)))
           - jnp.exp(jnp.sum(lq2.astype(jnp.float32) * lk2.astype(jnp.float32)))
           + lam_init)
    scale = DA_HEAD_DIM ** -0.5
    outs = []
    for blk in range(S // Q_BLOCK):
        s0 = blk * Q_BLOCK
        end = s0 + Q_BLOCK
        qb = q[:, :, :, s0:end]
        kb = k[:, :, :, :end]
        vb = v[:, :, :end]
        scores = jnp.einsum('bhcqd,bhckd->bhcqk', qb, kb).astype(jnp.float32) * scale
        causal = (s0 + jnp.arange(Q_BLOCK))[:, None] >= jnp.arange(end)[None, :]
        p = jax.nn.softmax(jnp.where(causal, scores, -jnp.inf), axis=-1)
        p_diff = (p[:, :, 0] - lam * p[:, :, 1]).astype(v.dtype)
        outs.append(jnp.einsum('bhqk,bhkd->bhqd', p_diff, vb))
    o = jnp.concatenate(outs, axis=2)
    o = rms_norm(o, sub_norm) * (1.0 - lam_init)
    o = o.transpose(0, 2, 1, 3).reshape(B, S, DA_V_WIDTH)
    return o @ w_out


def causal_depthwise_conv(x, w):
    C = x.shape[-1]
    return lax.conv_general_dilated(
        x, w[:, None, :].astype(x.dtype), window_strides=(1,),
        padding=[(w.shape[0] - 1, 0)],
        dimension_numbers=('NWC', 'WIO', 'NWC'),
        feature_group_count=C)


def gated_delta_rule(q, k, v, beta, g):
    B, H, S, dk = q.shape
    dv = v.shape[-1]
    C = GD_CHUNK
    N = S // C
    q = q.reshape(B, H, N, C, dk)
    k = k.reshape(B, H, N, C, dk)
    v = v.reshape(B, H, N, C, dv)
    beta = beta.reshape(B, H, N, C)
    g = jnp.cumsum(g.reshape(B, H, N, C), axis=-1)
    incl = jnp.tril(jnp.ones((C, C), dtype=bool))
    strict = jnp.tril(jnp.ones((C, C), dtype=bool), -1)
    decay = jnp.exp(jnp.where(incl, g[..., :, None] - g[..., None, :], -jnp.inf))
    kb = k * beta[..., None]
    vb = v * beta[..., None]
    L = jnp.where(strict, jnp.einsum('bhncd,bhnsd->bhncs', kb, k) * decay, 0.0)
    eye = jnp.eye(C, dtype=L.dtype)
    T = lax.linalg.triangular_solve(L + eye, jnp.broadcast_to(eye, L.shape),
                                    left_side=True, lower=True, unit_diagonal=True)
    u = jnp.einsum('bhncs,bhnse->bhnce', T, vb)
    w = jnp.einsum('bhncs,bhnsd->bhncd', T, kb * jnp.exp(g)[..., None])
    a_qk = jnp.where(incl, jnp.einsum('bhncd,bhnsd->bhncs', q, k) * decay, 0.0)
    q_dec = q * jnp.exp(g)[..., None]
    k_dec = k * jnp.exp(g[..., -1:] - g)[..., None]
    g_last = jnp.exp(g[..., -1])

    def step(state, xs):
        u_n, w_n, a_n, qd_n, kd_n, gl_n = xs
        v_new = u_n - jnp.einsum('bhcd,bhde->bhce', w_n, state)
        o_n = (jnp.einsum('bhcd,bhde->bhce', qd_n, state)
               + jnp.einsum('bhcs,bhse->bhce', a_n, v_new))
        state = state * gl_n[..., None, None] + jnp.einsum('bhcd,bhce->bhde', kd_n, v_new)
        return state, o_n

    xs = tuple(jnp.moveaxis(t, 2, 0) for t in (u, w, a_qk, q_dec, k_dec, g_last))
    state0 = jnp.zeros((B, H, dk, dv), jnp.float32)
    _, o = lax.scan(step, state0, xs)
    return jnp.moveaxis(o, 0, 2).reshape(B, H, S, dv)


def gated_deltanet(xn, w_in, conv_w, a_log, dt_bias, out_norm, w_out):
    B, S, _ = xn.shape
    proj = xn @ w_in
    qkv = jax.nn.silu(causal_depthwise_conv(proj[..., :3 * GD_WIDTH], conv_w))
    z = proj[..., 3 * GD_WIDTH:4 * GD_WIDTH]
    b = proj[..., 4 * GD_WIDTH:4 * GD_WIDTH + GD_HEADS]
    a = proj[..., 4 * GD_WIDTH + GD_HEADS:]

    def heads(t):
        return t.reshape(B, S, GD_HEADS, GD_HEAD_DIM).transpose(0, 2, 1, 3).astype(jnp.float32)

    q = l2_norm(heads(qkv[..., :GD_WIDTH])) * (GD_HEAD_DIM ** -0.5)
    k = l2_norm(heads(qkv[..., GD_WIDTH:2 * GD_WIDTH]))
    v = heads(qkv[..., 2 * GD_WIDTH:])
    beta = jax.nn.sigmoid(b.astype(jnp.float32)).transpose(0, 2, 1)
    g = (-jnp.exp(a_log.astype(jnp.float32))
         * jax.nn.softplus(a.astype(jnp.float32) + dt_bias.astype(jnp.float32))).transpose(0, 2, 1)
    o = gated_delta_rule(q, k, v, beta, g).transpose(0, 2, 1, 3)
    o = rms_norm(o, out_norm) * jax.nn.silu(z.reshape(B, S, GD_HEADS, GD_HEAD_DIM).astype(jnp.float32))
    return o.astype(xn.dtype).reshape(B, S, GD_WIDTH) @ w_out


def squared_relu_mlp(xn, w1, w2):
    h = jax.nn.relu(xn @ w1)
    return (h * h) @ w2


def setup_inputs(seed: int = 0) -> dict:
    key = jax.random.key(seed)
    ks = jax.random.split(key, 24)
    n_attn = (DEPTH + N_MIXERS - 1) // N_MIXERS
    n_delta = DEPTH // N_MIXERS
    f32 = jnp.float32

    def nrm(k, shape, scale):
        return jax.random.normal(k, shape, f32) * scale

    def gain(k, shape):
        return 1.0 + 0.02 * jax.random.normal(k, shape, f32)

    dt = jnp.exp(jax.random.uniform(ks[17], (n_delta, GD_HEADS), f32,
                                    minval=math.log(1e-3), maxval=math.log(1e-1)))
    return {
        "x": jax.random.normal(ks[0], (BATCH, SEQ, D_MODEL), f32),
        "mix_norm": gain(ks[1], (DEPTH, D_MODEL)),
        "mlp_norm": gain(ks[2], (DEPTH, D_MODEL)),
        "mlp_w_in": nrm(ks[3], (DEPTH, D_MODEL, D_FF), D_MODEL ** -0.5),
        "mlp_w_out": nrm(ks[4], (DEPTH, D_FF, D_MODEL), D_FF ** -0.5),
        "da_w_in": nrm(ks[5], (n_attn, D_MODEL, 2 * DA_QK_WIDTH + DA_V_WIDTH), D_MODEL ** -0.5),
        "da_q_norm": gain(ks[6], (n_attn, DA_HEAD_DIM)),
        "da_k_norm": gain(ks[7], (n_attn, DA_HEAD_DIM)),
        "da_lambda_q1": nrm(ks[8], (n_attn, DA_HEAD_DIM), 0.1),
        "da_lambda_k1": nrm(ks[9], (n_attn, DA_HEAD_DIM), 0.1),
        "da_lambda_q2": nrm(ks[10], (n_attn, DA_HEAD_DIM), 0.1),
        "da_lambda_k2": nrm(ks[11], (n_attn, DA_HEAD_DIM), 0.1),
        "da_sub_norm": gain(ks[12], (n_attn, DA_V_DIM)),
        "da_w_out": nrm(ks[13], (n_attn, DA_V_WIDTH, D_MODEL), DA_V_WIDTH ** -0.5),
        "gd_w_in": nrm(ks[14], (n_delta, D_MODEL, GD_IN_WIDTH), D_MODEL ** -0.5),
        "gd_conv_w": nrm(ks[15], (n_delta, GD_CONV, 3 * GD_WIDTH), GD_CONV ** -0.5),
        "gd_a_log": jnp.log(jax.random.uniform(ks[16], (n_delta, GD_HEADS), f32, minval=1.0, maxval=16.0)),
        "gd_dt_bias": dt + jnp.log(-jnp.expm1(-dt)),
        "gd_out_norm": gain(ks[18], (n_delta, GD_HEAD_DIM)),
        "gd_w_out": nrm(ks[19], (n_delta, GD_WIDTH, D_MODEL), GD_WIDTH ** -0.5),
    }


def reference(x, mix_norm, mlp_norm, mlp_w_in, mlp_w_out,
              da_w_in, da_q_norm, da_k_norm, da_lambda_q1, da_lambda_k1,
              da_lambda_q2, da_lambda_k2, da_sub_norm, da_w_out,
              gd_w_in, gd_conv_w, gd_a_log, gd_dt_bias, gd_out_norm, gd_w_out):
    for i in range(DEPTH):
        j = i // N_MIXERS
        h = rms_norm(x, mix_norm[i])
        if i % N_MIXERS == 0:
            h = diff_attention(h, da_w_in[j], da_q_norm[j], da_k_norm[j],
                               da_lambda_q1[j], da_lambda_k1[j], da_lambda_q2[j], da_lambda_k2[j],
                               da_sub_norm[j], da_w_out[j], i)
        else:
            h = gated_deltanet(h, gd_w_in[j], gd_conv_w[j], gd_a_log[j], gd_dt_bias[j],
                               gd_out_norm[j], gd_w_out[j])
        x = x + h
        x = x + squared_relu_mlp(rms_norm(x, mlp_norm[i]), mlp_w_in[i], mlp_w_out[i])
    return x
```

```python
import functools
import math

import jax
import jax.numpy as jnp
from jax import lax
from jax.experimental import pallas as pl
from jax.experimental.pallas import tpu as pltpu

F32 = jnp.float32
BF16 = jnp.bfloat16

EPS = 1e-6
LANES = 128
SUBLANES = 8
VMEM_LIMIT_BYTES = 56 * 1024 * 1024

DA_HEAD_DIM = 64
GD_CONV = 4
GD_CHUNK = 128
INV_BASE = 16

TOKEN_TILE = 512
COL_TILE = 512
ATTN_TILE = 256
FF_TILE = 1024


def _params(*sem):
    return pltpu.CompilerParams(dimension_semantics=sem, vmem_limit_bytes=VMEM_LIMIT_BYTES)


def _resident(shape):
    nd = len(shape)
    return pl.BlockSpec(shape, lambda *_: (0,) * nd, pipeline_mode=pl.Buffered(1))


def _rms_rows(x, gain):
    return x * lax.rsqrt(jnp.mean(x * x, axis=-1, keepdims=True) + EPS) * gain


def _dot(a, b):
    return jnp.dot(a, b, preferred_element_type=F32)


def _dot_nt(a, b):
    return lax.dot_general(a, b, (((1,), (1,)), ((), ())), preferred_element_type=F32)


def _dot_tn(a, b):
    return lax.dot_general(a, b, (((0,), (0,)), ((), ())), preferred_element_type=F32)


def _split(a):
    hi = a.astype(BF16)
    lo = (a - hi.astype(F32)).astype(BF16)
    return hi, lo


def _dot3(a, b):
    ah, al = _split(a)
    bh, bl = _split(b)
    return _dot(ah, bh) + (_dot(ah, bl) + _dot(al, bh))


def _da_inproj_kernel(x_ref, g_ref, w_ref, qg_ref, kg_ref, o_ref, *, qk_width):
    xn = _rms_rows(x_ref[...], g_ref[...]).astype(BF16)
    n_out = o_ref.shape[1]
    row = lax.broadcasted_iota(jnp.int32, (LANES, LANES), 0)
    col = lax.broadcasted_iota(jnp.int32, (LANES, LANES), 1)
    same_half = ((row // DA_HEAD_DIM) == (col // DA_HEAD_DIM)).astype(BF16)
    for c0 in range(0, n_out, COL_TILE):
        acc = _dot(xn, w_ref[:, c0:c0 + COL_TILE])
        if c0 >= 2 * qk_width:
            o_ref[:, c0:c0 + COL_TILE] = acc.astype(o_ref.dtype)
            continue
        gain = qg_ref[...] if c0 < qk_width else kg_ref[...]
        for h0 in range(0, COL_TILE, LANES):
            blk = acc[:, h0:h0 + LANES]
            hi, lo = _split(blk * blk)
            ss = _dot(hi, same_half) + _dot(lo, same_half)
            y = blk * lax.rsqrt(ss * (1.0 / DA_HEAD_DIM) + EPS) * gain
            o_ref[:, c0 + h0:c0 + h0 + LANES] = y.astype(o_ref.dtype)


def _da_inproj(x, gain, w, q_gain, k_gain, qk_width):
    t, d = x.shape
    n = w.shape[1]
    return pl.pallas_call(
        functools.partial(_da_inproj_kernel, qk_width=qk_width),
        grid=(t // TOKEN_TILE,),
        in_specs=[
            pl.BlockSpec((TOKEN_TILE, d), lambda i: (i, 0)),
            _resident((1, d)),
            _resident((d, n)),
            _resident((1, LANES)),
            _resident((1, LANES)),
        ],
        out_specs=pl.BlockSpec((TOKEN_TILE, n), lambda i: (i, 0)),
        out_shape=jax.ShapeDtypeStruct((t, n), BF16),
        compiler_params=_params("parallel"),
        name="da_inproj",
    )(x, gain, w, q_gain, k_gain)


def _diff_attn_kernel(lam_ref, sn_ref, q_ref, k_ref, v_ref, o_ref, *, lam_init):
    tq = q_ref.shape[1]
    i = pl.program_id(2)

    lv = lam_ref[...]
    lam = (jnp.exp(jnp.sum(lv[0:1] * lv[1:2], axis=-1, keepdims=True))
           - jnp.exp(jnp.sum(lv[2:3] * lv[3:4], axis=-1, keepdims=True)) + lam_init)

    q = q_ref[0]
    lane = lax.broadcasted_iota(jnp.int32, q.shape, 1)
    zero = jnp.zeros_like(q)
    qs = jnp.concatenate([jnp.where(lane < DA_HEAD_DIM, q, zero),
                          jnp.where(lane >= DA_HEAD_DIM, q, zero)], axis=0)

    def block(j, carry, diagonal):
        m, l, acc = carry
        start = pl.multiple_of(j * tq, tq)
        k = k_ref[0, pl.ds(start, tq), :]
        v = v_ref[0, pl.ds(start, tq), :]
        s = _dot_nt(qs, k)
        if diagonal:
            r = lax.broadcasted_iota(jnp.int32, s.shape, 0)
            c = lax.broadcasted_iota(jnp.int32, s.shape, 1)
            r = jnp.where(r >= tq, r - tq, r)
            s = jnp.where(r >= c, s, -jnp.inf)
        m_new = jnp.maximum(m, jnp.max(s, axis=-1, keepdims=True))
        alpha = jnp.exp(m - m_new)
        p = jnp.exp(s - m_new)
        l = alpha * l + jnp.sum(p, axis=-1, keepdims=True)
        acc = alpha * acc + _dot(p.astype(BF16), v)
        return m_new, l, acc

    init = (jnp.full((2 * tq, 1), -jnp.inf, F32), jnp.zeros((2 * tq, 1), F32),
            jnp.zeros((2 * tq, LANES), F32))
    carry = lax.fori_loop(0, i, lambda j, c: block(j, c, False), init)
    _, l, acc = block(i, carry, True)

    o = acc / l
    o = o[:tq] - lam * o[tq:]
    o = _rms_rows(o, sn_ref[...]) * (1.0 - lam_init)
    o_ref[0] = o.astype(o_ref.dtype)


def _diff_attn(qkv, lam_vecs, sub_gain, lam_init, heads):
    b, s, _ = qkv.shape
    tq = ATTN_TILE
    return pl.pallas_call(
        functools.partial(_diff_attn_kernel, lam_init=lam_init),
        grid=(b, heads, s // tq),
        in_specs=[
            _resident(lam_vecs.shape),
            _resident((1, LANES)),
            pl.BlockSpec((1, tq, LANES), lambda bi, h, i: (bi, i, h)),
            pl.BlockSpec((1, s, LANES), lambda bi, h, i: (bi, 0, heads + h)),
            pl.BlockSpec((1, s, LANES), lambda bi, h, i: (bi, 0, 2 * heads + h)),
        ],
        out_specs=pl.BlockSpec((1, tq, LANES), lambda bi, h, i: (bi, i, h)),
        out_shape=jax.ShapeDtypeStruct((b, s, heads * LANES), BF16),
        compiler_params=_params("parallel", "parallel", "arbitrary"),
        name="diff_attn",
    )(lam_vecs, sub_gain, qkv, qkv, qkv)


def _gd_inproj_kernel(x_ref, g_ref, wqkv_ref, wz_ref, wba_ref, cw_ref, alog_ref, dtb_ref,
                      qkv_ref, z_ref, gb_ref, ext_ref, *, steps_per_seq, width, heads, q_scale):
    tm = x_ref.shape[0]
    i = pl.program_id(0)
    halo = SUBLANES

    @pl.when(i % steps_per_seq == 0)
    def _():
        ext_ref[0:halo, :] = jnp.zeros((halo, ext_ref.shape[1]), F32)

    xn = _rms_rows(x_ref[...], g_ref[...]).astype(BF16)

    for c0 in range(0, 3 * width, COL_TILE):
        cols = slice(c0, c0 + COL_TILE)
        acc = _dot(xn, wqkv_ref[:, cols])
        ext_ref[halo:halo + tm, cols] = acc
        y = acc * cw_ref[GD_CONV - 1:GD_CONV, cols]
        for back in range(1, GD_CONV):
            tap = GD_CONV - 1 - back
            y = y + ext_ref[halo - back:halo - back + tm, cols] * cw_ref[tap:tap + 1, cols]
        y = y * jax.nn.sigmoid(y)
        if c0 >= 2 * width:
            qkv_ref[:, cols] = y.astype(qkv_ref.dtype)
            continue
        post = q_scale if c0 < width else 1.0
        for h0 in range(0, COL_TILE, LANES):
            blk = y[:, h0:h0 + LANES]
            blk = blk * lax.rsqrt(jnp.sum(blk * blk, axis=-1, keepdims=True) + EPS)
            if post != 1.0:
                blk = blk * post
            qkv_ref[:, c0 + h0:c0 + h0 + LANES] = blk.astype(qkv_ref.dtype)

    ext_ref[0:halo, :] = ext_ref[tm:tm + halo, :]

    for c0 in range(0, width, COL_TILE):
        z_ref[:, c0:c0 + COL_TILE] = _dot(xn, wz_ref[:, c0:c0 + COL_TILE]).astype(z_ref.dtype)

    ba = _dot(xn, wba_ref[...])
    beta = jax.nn.sigmoid(ba)
    a = ba + dtb_ref[...]
    softplus = jnp.maximum(a, 0.0) + jnp.log1p(jnp.exp(-jnp.abs(a)))
    g = -jnp.exp(alog_ref[...]) * softplus
    lane = lax.broadcasted_iota(jnp.int32, (GD_CHUNK, LANES), 1)
    r = lax.broadcasted_iota(jnp.int32, (GD_CHUNK, GD_CHUNK), 0)
    c = lax.broadcasted_iota(jnp.int32, (GD_CHUNK, GD_CHUNK), 1)
    tril = (r >= c).astype(F32)
    for r0 in range(0, tm, GD_CHUNK):
        gc = jnp.dot(tril, g[r0:r0 + GD_CHUNK], preferred_element_type=F32,
                     precision=lax.Precision.HIGHEST)
        gb_ref[r0:r0 + GD_CHUNK, :] = jnp.where(lane < heads, beta[r0:r0 + GD_CHUNK], gc)


def _gd_inproj(x, gain, wqkv, wz, wba, conv_w, alog, dtb, seq, heads, q_scale):
    t, d = x.shape
    width = wz.shape[1]
    kern = functools.partial(_gd_inproj_kernel, steps_per_seq=seq // TOKEN_TILE, width=width,
                             heads=heads, q_scale=q_scale)
    return pl.pallas_call(
        kern,
        grid=(t // TOKEN_TILE,),
        in_specs=[
            pl.BlockSpec((TOKEN_TILE, d), lambda i: (i, 0)),
            _resident((1, d)),
            _resident(wqkv.shape),
            _resident(wz.shape),
            _resident(wba.shape),
            _resident(conv_w.shape),
            _resident((1, LANES)),
            _resident((1, LANES)),
        ],
        out_specs=[
            pl.BlockSpec((TOKEN_TILE, 3 * width), lambda i: (i, 0)),
            pl.BlockSpec((TOKEN_TILE, width), lambda i: (i, 0)),
            pl.BlockSpec((TOKEN_TILE, LANES), lambda i: (i, 0)),
        ],
        out_shape=[
            jax.ShapeDtypeStruct((t, 3 * width), BF16),
            jax.ShapeDtypeStruct((t, width), BF16),
            jax.ShapeDtypeStruct((t, LANES), F32),
        ],
        scratch_shapes=[pltpu.VMEM((TOKEN_TILE + 2 * SUBLANES, 3 * width), F32)],
        compiler_params=_params("arbitrary"),
        name="gd_inproj",
    )(x, gain, wqkv, wz, wba, conv_w, alog, dtb)


def _unit_lower_inverse(low, r, c):
    n = low.shape[0]
    eye = (r == c).astype(F32)
    same = (r // INV_BASE) == (c // INV_BASE)
    m = jnp.where(same, -low, 0.0)
    inv = eye + m
    power = m
    span = 2
    while span < INV_BASE:
        power = _dot3(power, power)
        inv = inv + _dot3(inv, power)
        span *= 2
    size = INV_BASE
    while size < n:
        pair = (r // (2 * size)) == (c // (2 * size))
        off = jnp.where(pair & jnp.logical_not(same), low, 0.0)
        inv = inv - _dot3(inv, _dot3(off, inv))
        same = pair
        size *= 2
    return inv


def _gdn_kernel(rows_ref, on_ref, q_ref, k_ref, v_ref, z_ref, o_ref):
    ck = GD_CHUNK
    n_chunks = q_ref.shape[1] // ck
    r = lax.broadcasted_iota(jnp.int32, (ck, ck), 0)
    c = lax.broadcasted_iota(jnp.int32, (ck, ck), 1)
    incl = r >= c
    strict = r > c

    def body(n, state):
        start = pl.multiple_of(n * ck, ck)
        q = q_ref[0, pl.ds(start, ck), :]
        k = k_ref[0, pl.ds(start, ck), :]
        v = v_ref[0, pl.ds(start, ck), :].astype(F32)
        beta_r = jnp.broadcast_to(rows_ref[0, 0, 0, pl.ds(n, 1), :], (ck, ck))
        g_r = jnp.broadcast_to(rows_ref[0, 1, 0, pl.ds(n, 1), :], (ck, ck))
        beta_c = beta_r.T
        g_c = g_r.T
        g_last = g_r[:, ck - 1:ck]

        decay = jnp.exp(jnp.where(incl, g_c - g_r, -jnp.inf))
        kf = k.astype(F32)
        kb = kf * beta_c
        low = jnp.where(strict, _dot_nt(kb.astype(BF16), k) * decay, 0.0)
        inv = _unit_lower_inverse(low, r, c)
        rhs = jnp.concatenate([v * beta_c, kb * jnp.exp(g_c)], axis=1).astype(BF16)
        uw = _dot(inv.astype(BF16), rhs)
        u, w = uw[:, :LANES], uw[:, LANES:]
        a_qk = jnp.where(incl, _dot_nt(q, k) * decay, 0.0)
        q_dec = q.astype(F32) * jnp.exp(g_c)
        k_dec = kf * jnp.exp(g_last - g_c)

        sb = state.astype(BF16)
        v_new = u - _dot(w.astype(BF16), sb)
        vb16 = v_new.astype(BF16)
        o = _dot(q_dec.astype(BF16), sb) + _dot(a_qk.astype(BF16), vb16)
        state = state * jnp.exp(g_last) + _dot_tn(k_dec.astype(BF16), vb16)

        o = _rms_rows(o, on_ref[...])
        zf = z_ref[0, pl.ds(start, ck), :].astype(F32)
        o_ref[0, pl.ds(start, ck), :] = (o * (zf * jax.nn.sigmoid(zf))).astype(o_ref.dtype)
        return state

    lax.fori_loop(0, n_chunks, body, jnp.zeros((LANES, LANES), F32))


def _gdn(qkv, z, rows, out_gain, heads):
    b, s, _ = qkv.shape
    n_chunks = s // GD_CHUNK
    return pl.pallas_call(
        _gdn_kernel,
        grid=(b, heads),
        in_specs=[
            pl.BlockSpec((1, 2, 1, n_chunks, GD_CHUNK), lambda bi, h: (bi, 0, h, 0, 0)),
            _resident((1, LANES)),
            pl.BlockSpec((1, s, LANES), lambda bi, h: (bi, 0, h)),
            pl.BlockSpec((1, s, LANES), lambda bi, h: (bi, 0, heads + h)),
            pl.BlockSpec((1, s, LANES), lambda bi, h: (bi, 0, 2 * heads + h)),
            pl.BlockSpec((1, s, LANES), lambda bi, h: (bi, 0, h)),
        ],
        out_specs=pl.BlockSpec((1, s, LANES), lambda bi, h: (bi, 0, h)),
        out_shape=jax.ShapeDtypeStruct((b, s, heads * LANES), BF16),
        compiler_params=_params("parallel", "parallel"),
        name="gdn",
    )(rows, out_gain, qkv, qkv, qkv, z)


def _out_mlp_kernel(x_ref, o_ref, wo_ref, g_ref, w1_ref, w2_ref, y_ref):
    x1 = x_ref[...] + _dot(o_ref[...], wo_ref[...])
    xn = _rms_rows(x1, g_ref[...]).astype(BF16)
    acc = x1
    for f0 in range(0, w1_ref.shape[1], FF_TILE):
        h = jnp.maximum(_dot(xn, w1_ref[:, f0:f0 + FF_TILE]), 0.0)
        acc = acc + _dot((h * h).astype(BF16), w2_ref[f0:f0 + FF_TILE, :])
    y_ref[...] = acc


def _out_mlp(x, o, wo, gain, w1, w2):
    t, d = x.shape
    return pl.pallas_call(
        _out_mlp_kernel,
        grid=(t // TOKEN_TILE,),
        in_specs=[
            pl.BlockSpec((TOKEN_TILE, d), lambda i: (i, 0)),
            pl.BlockSpec((TOKEN_TILE, o.shape[1]), lambda i: (i, 0)),
            _resident(wo.shape),
            _resident((1, d)),
            _resident(w1.shape),
            _resident(w2.shape),
        ],
        out_specs=pl.BlockSpec((TOKEN_TILE, d), lambda i: (i, 0)),
        out_shape=jax.ShapeDtypeStruct((t, d), F32),
        compiler_params=_params("parallel"),
        name="out_mlp",
    )(x, o, wo, gain, w1, w2)


def _lane_row(vec, offset=0):
    return jnp.zeros((1, LANES), F32).at[0, offset:offset + vec.shape[0]].set(vec.astype(F32))


def kernel(x, mix_norm, mlp_norm, mlp_w_in, mlp_w_out, da_w_in, da_q_norm, da_k_norm, da_lambda_q1, da_lambda_k1, da_lambda_q2, da_lambda_k2, da_sub_norm, da_w_out, gd_w_in, gd_conv_w, gd_a_log, gd_dt_bias, gd_out_norm, gd_w_out):
    batch, seq, d = x.shape
    depth = mix_norm.shape[0]
    n_mixers = 2
    da_heads = da_w_out.shape[1] // LANES
    gd_width = gd_w_out.shape[1]
    gd_heads = gd_a_log.shape[1]
    assert seq % TOKEN_TILE == 0 and seq % ATTN_TILE == 0 and seq % GD_CHUNK == 0
    assert gd_width // gd_heads == LANES and da_w_out.shape[1] // da_heads == LANES

    xf = x.reshape(batch * seq, d)
    for i in range(depth):
        j = i // n_mixers
        gain = mix_norm[i].reshape(1, d)
        if i % n_mixers == 0:
            qk_width = da_heads * 2 * DA_HEAD_DIM
            q_gain = jnp.tile(da_q_norm[j], 2).reshape(1, LANES) * (DA_HEAD_DIM ** -0.5)
            k_gain = jnp.tile(da_k_norm[j], 2).reshape(1, LANES)
            qkv = _da_inproj(xf, gain, da_w_in[j].astype(BF16), q_gain, k_gain, qk_width)
            lam_vecs = jnp.stack([da_lambda_q1[j], da_lambda_k1[j], da_lambda_q2[j], da_lambda_k2[j]])
            lam_init = 0.8 - 0.6 * math.exp(-0.3 * i)
            o = _diff_attn(qkv.reshape(batch, seq, -1), lam_vecs.astype(F32),
                           da_sub_norm[j].reshape(1, LANES), lam_init, da_heads)
            w_out = da_w_out[j]
        else:
            w_in = gd_w_in[j]
            wqkv = w_in[:, :3 * gd_width].astype(BF16)
            wz = w_in[:, 3 * gd_width:4 * gd_width].astype(BF16)
            wba = jnp.zeros((d, LANES), BF16).at[:, :2 * gd_heads].set(w_in[:, 4 * gd_width:].astype(BF16))
            qkv, z, gb = _gd_inproj(xf, gain, wqkv, wz, wba, gd_conv_w[j],
                                    _lane_row(gd_a_log[j], gd_heads), _lane_row(gd_dt_bias[j], gd_heads),
                                    seq, gd_heads, LANES ** -0.5)
            rows = gb[:, :2 * gd_heads].reshape(batch, seq // GD_CHUNK, GD_CHUNK, 2, gd_heads)
            rows = rows.transpose(0, 3, 4, 1, 2)
            o = _gdn(qkv.reshape(batch, seq, -1), z.reshape(batch, seq, -1), rows,
                     gd_out_norm[j].reshape(1, LANES), gd_heads)
            w_out = gd_w_out[j]
        xf = _out_mlp(xf, o.reshape(batch * seq, -1), w_out.astype(BF16), mlp_norm[i].reshape(1, d),
                      mlp_w_in[i].astype(BF16), mlp_w_out[i].astype(BF16))
    return xf.reshape(batch, seq, d)
```

```python
import functools
import math

import jax
import jax.numpy as jnp
from jax import lax
from jax.experimental import pallas as pl
from jax.experimental.pallas import tpu as pltpu

F32 = jnp.float32
BF16 = jnp.bfloat16

EPS = 1e-6
LANES = 128
SUBLANES = 8
VMEM_LIMIT_BYTES = 56 * 1024 * 1024

DA_HEAD_DIM = 64
GD_CONV = 4
GD_CHUNK = 128
INV_BASE = 16
DA_HEAD_GROUP = 4
GD_HEAD_GROUP = 8

TOKEN_TILE = 512
COL_TILE = 512
ATTN_TILE = 256
FF_TILE = 1024


def _params(*sem):
    return pltpu.CompilerParams(dimension_semantics=sem, vmem_limit_bytes=VMEM_LIMIT_BYTES)


def _resident(shape):
    nd = len(shape)
    return pl.BlockSpec(shape, lambda *_: (0,) * nd, pipeline_mode=pl.Buffered(1))


def _rms_rows(x, gain):
    return x * lax.rsqrt(jnp.mean(x * x, axis=-1, keepdims=True) + EPS) * gain


def _dot(a, b):
    return jnp.dot(a, b, preferred_element_type=F32)


def _dot_nt(a, b):
    return lax.dot_general(a, b, (((1,), (1,)), ((), ())), preferred_element_type=F32)


def _dot_tn(a, b):
    return lax.dot_general(a, b, (((0,), (0,)), ((), ())), preferred_element_type=F32)


def _each(fn, *per_head):
    return [fn(*args) for args in zip(*per_head)]


def _split(a):
    hi = a.astype(BF16)
    lo = (a - hi.astype(F32)).astype(BF16)
    return hi, lo


def _dot3(a, b):
    ah, al = _split(a)
    bh, bl = _split(b)
    return _dot(ah, bh) + (_dot(ah, bl) + _dot(al, bh))


def _da_inproj_kernel(x_ref, g_ref, w_ref, qg_ref, kg_ref, qk_ref, vt_ref):
    xn = _rms_rows(x_ref[...], g_ref[...]).astype(BF16)
    tm = x_ref.shape[0]
    n_qk = qk_ref.shape[1]
    qk_width = n_qk // 2
    row = lax.broadcasted_iota(jnp.int32, (LANES, LANES), 0)
    col = lax.broadcasted_iota(jnp.int32, (LANES, LANES), 1)
    same_half = ((row // DA_HEAD_DIM) == (col // DA_HEAD_DIM)).astype(BF16)
    for c0 in range(0, w_ref.shape[1], COL_TILE):
        acc = _dot(xn, w_ref[:, c0:c0 + COL_TILE])
        if c0 >= n_qk:
            for h0 in range(0, COL_TILE, LANES):
                head = (c0 - n_qk + h0) // LANES
                for t0 in range(0, tm, ATTN_TILE):
                    vt_ref[0, head, t0 // ATTN_TILE] = acc[t0:t0 + ATTN_TILE, h0:h0 + LANES].T.astype(vt_ref.dtype)
            continue
        gain = qg_ref[...] if c0 < qk_width else kg_ref[...]
        for h0 in range(0, COL_TILE, LANES):
            blk = acc[:, h0:h0 + LANES]
            hi, lo = _split(blk * blk)
            ss = _dot(hi, same_half) + _dot(lo, same_half)
            y = blk * lax.rsqrt(ss * (1.0 / DA_HEAD_DIM) + EPS) * gain
            qk_ref[:, c0 + h0:c0 + h0 + LANES] = y.astype(qk_ref.dtype)


def _da_inproj(x, gain, w, q_gain, k_gain, qk_width, batch, seq):
    t, d = x.shape
    n = w.shape[1]
    heads = (n - 2 * qk_width) // LANES
    per_seq = seq // TOKEN_TILE
    tiles = TOKEN_TILE // ATTN_TILE
    return pl.pallas_call(
        _da_inproj_kernel,
        grid=(t // TOKEN_TILE,),
        in_specs=[
            pl.BlockSpec((TOKEN_TILE, d), lambda i: (i, 0)),
            _resident((1, d)),
            _resident((d, n)),
            _resident((1, LANES)),
            _resident((1, LANES)),
        ],
        out_specs=[
            pl.BlockSpec((TOKEN_TILE, 2 * qk_width), lambda i: (i, 0)),
            pl.BlockSpec((1, heads, tiles, LANES, ATTN_TILE), lambda i: (i // per_seq, 0, i % per_seq, 0, 0)),
        ],
        out_shape=[
            jax.ShapeDtypeStruct((t, 2 * qk_width), BF16),
            jax.ShapeDtypeStruct((batch, heads, seq // ATTN_TILE, LANES, ATTN_TILE), BF16),
        ],
        compiler_params=_params("parallel"),
        name="da_inproj",
    )(x, gain, w, q_gain, k_gain)


def _diff_attn_kernel(lam_ref, sn_ref, q_ref, k_ref, vt_ref, o_ref, *, lam_init):
    tq = q_ref.shape[1]
    i = pl.program_id(2)

    lv = lam_ref[...]
    lam = (jnp.exp(jnp.sum(lv[0:1] * lv[1:2], axis=-1, keepdims=True))
           - jnp.exp(jnp.sum(lv[2:3] * lv[3:4], axis=-1, keepdims=True)) + lam_init)

    heads = range(q_ref.shape[2] // LANES)
    lanes = [slice(hd * LANES, (hd + 1) * LANES) for hd in heads]

    def stacked_queries(ln):
        qt = q_ref[0, :, ln].astype(F32).T
        feat = lax.broadcasted_iota(jnp.int32, qt.shape, 0)
        return jnp.concatenate([jnp.where(feat < DA_HEAD_DIM, qt, 0.0),
                                jnp.where(feat >= DA_HEAD_DIM, qt, 0.0)], axis=1).astype(BF16)

    qsts = [stacked_queries(ln) for ln in lanes]

    def block(j, carry, diagonal):
        ms, ls, accs = carry
        rows = pl.ds(pl.multiple_of(j * tq, tq), tq)
        ss = [_dot(k_ref[0, rows, ln], qst) for ln, qst in zip(lanes, qsts)]
        if diagonal:
            key = lax.broadcasted_iota(jnp.int32, ss[0].shape, 0)
            qry = lax.broadcasted_iota(jnp.int32, ss[0].shape, 1)
            keep = jnp.where(qry >= tq, qry - tq, qry) >= key
            ss = _each(lambda s: jnp.where(keep, s, -jnp.inf), ss)
        m_news = _each(lambda m, s: jnp.maximum(m, jnp.max(s, axis=0, keepdims=True)), ms, ss)
        alphas = _each(lambda m, mn: jnp.exp(m - mn), ms, m_news)
        ps = _each(lambda s, mn: jnp.exp(s - mn), ss, m_news)
        ls = _each(lambda a, l, p: a * l + jnp.sum(p, axis=0, keepdims=True), alphas, ls, ps)
        pvs = [_dot(vt_ref[0, hd, j], p.astype(BF16)) for hd, p in zip(heads, ps)]
        accs = _each(lambda a, acc, pv: a * acc + pv, alphas, accs, pvs)
        return m_news, ls, accs

    init = ([jnp.full((1, 2 * tq), -jnp.inf, F32) for _ in heads],
            [jnp.zeros((1, 2 * tq), F32) for _ in heads],
            [jnp.zeros((LANES, 2 * tq), F32) for _ in heads])
    carry = lax.fori_loop(0, i, lambda j, c: block(j, c, False), init)
    _, ls, accs = block(i, carry, True)

    for ln, l, acc in zip(lanes, ls, accs):
        ot = acc / l
        o = (ot[:, :tq] - lam * ot[:, tq:]).T
        o = _rms_rows(o, sn_ref[...]) * (1.0 - lam_init)
        o_ref[0, :, ln] = o.astype(o_ref.dtype)


def _diff_attn(qk, vt, lam_vecs, sub_gain, lam_init):
    b, s, _ = qk.shape
    heads, n_tiles = vt.shape[1], vt.shape[2]
    tq = ATTN_TILE
    grp = DA_HEAD_GROUP
    wide = grp * LANES
    ngrp = heads // grp
    return pl.pallas_call(
        functools.partial(_diff_attn_kernel, lam_init=lam_init),
        grid=(b, ngrp, s // tq),
        in_specs=[
            _resident(lam_vecs.shape),
            _resident((1, LANES)),
            pl.BlockSpec((1, tq, wide), lambda bi, h, i: (bi, i, h)),
            pl.BlockSpec((1, s, wide), lambda bi, h, i: (bi, 0, ngrp + h)),
            pl.BlockSpec((1, grp, n_tiles, LANES, tq), lambda bi, h, i: (bi, h, 0, 0, 0)),
        ],
        out_specs=pl.BlockSpec((1, tq, wide), lambda bi, h, i: (bi, i, h)),
        out_shape=jax.ShapeDtypeStruct((b, s, heads * LANES), BF16),
        compiler_params=_params("parallel", "parallel", "arbitrary"),
        name="diff_attn",
    )(lam_vecs, sub_gain, qk, qk, vt)


def _gd_inproj_kernel(x_ref, g_ref, wqkv_ref, wz_ref, wba_ref, cw_ref, alog_ref, dtb_ref,
                      qkv_ref, z_ref, gb_ref, ext_ref, *, steps_per_seq, width, heads, q_scale):
    tm = x_ref.shape[0]
    i = pl.program_id(0)
    halo = SUBLANES

    @pl.when(i % steps_per_seq == 0)
    def _():
        ext_ref[0:halo, :] = jnp.zeros((halo, ext_ref.shape[1]), F32)

    xn = _rms_rows(x_ref[...], g_ref[...]).astype(BF16)

    for c0 in range(0, 3 * width, COL_TILE):
        cols = slice(c0, c0 + COL_TILE)
        acc = _dot(xn, wqkv_ref[:, cols])
        ext_ref[halo:halo + tm, cols] = acc
        y = acc * cw_ref[GD_CONV - 1:GD_CONV, cols]
        for back in range(1, GD_CONV):
            tap = GD_CONV - 1 - back
            y = y + ext_ref[halo - back:halo - back + tm, cols] * cw_ref[tap:tap + 1, cols]
        y = y * jax.nn.sigmoid(y)
        if c0 >= 2 * width:
            qkv_ref[:, cols] = y.astype(qkv_ref.dtype)
            continue
        post = q_scale if c0 < width else 1.0
        for h0 in range(0, COL_TILE, LANES):
            blk = y[:, h0:h0 + LANES]
            blk = blk * lax.rsqrt(jnp.sum(blk * blk, axis=-1, keepdims=True) + EPS)
            if post != 1.0:
                blk = blk * post
            qkv_ref[:, c0 + h0:c0 + h0 + LANES] = blk.astype(qkv_ref.dtype)

    ext_ref[0:halo, :] = ext_ref[tm:tm + halo, :]

    for c0 in range(0, width, COL_TILE):
        z_ref[:, c0:c0 + COL_TILE] = _dot(xn, wz_ref[:, c0:c0 + COL_TILE]).astype(z_ref.dtype)

    ba = _dot(xn, wba_ref[...])
    beta = jax.nn.sigmoid(ba)
    a = ba + dtb_ref[...]
    softplus = jnp.maximum(a, 0.0) + jnp.log1p(jnp.exp(-jnp.abs(a)))
    g = -jnp.exp(alog_ref[...]) * softplus
    lane = lax.broadcasted_iota(jnp.int32, (GD_CHUNK, LANES), 1)
    r = lax.broadcasted_iota(jnp.int32, (GD_CHUNK, GD_CHUNK), 0)
    c = lax.broadcasted_iota(jnp.int32, (GD_CHUNK, GD_CHUNK), 1)
    tril = (r >= c).astype(F32)
    for r0 in range(0, tm, GD_CHUNK):
        gc = jnp.dot(tril, g[r0:r0 + GD_CHUNK], preferred_element_type=F32,
                     precision=lax.Precision.HIGHEST)
        gb_ref[r0:r0 + GD_CHUNK, :] = jnp.where(lane < heads, beta[r0:r0 + GD_CHUNK], gc)


def _gd_inproj(x, gain, wqkv, wz, wba, conv_w, alog, dtb, seq, heads, q_scale):
    t, d = x.shape
    width = wz.shape[1]
    kern = functools.partial(_gd_inproj_kernel, steps_per_seq=seq // TOKEN_TILE, width=width,
                             heads=heads, q_scale=q_scale)
    return pl.pallas_call(
        kern,
        grid=(t // TOKEN_TILE,),
        in_specs=[
            pl.BlockSpec((TOKEN_TILE, d), lambda i: (i, 0)),
            _resident((1, d)),
            _resident(wqkv.shape),
            _resident(wz.shape),
            _resident(wba.shape),
            _resident(conv_w.shape),
            _resident((1, LANES)),
            _resident((1, LANES)),
        ],
        out_specs=[
            pl.BlockSpec((TOKEN_TILE, 3 * width), lambda i: (i, 0)),
            pl.BlockSpec((TOKEN_TILE, width), lambda i: (i, 0)),
            pl.BlockSpec((TOKEN_TILE, LANES), lambda i: (i, 0)),
        ],
        out_shape=[
            jax.ShapeDtypeStruct((t, 3 * width), BF16),
            jax.ShapeDtypeStruct((t, width), BF16),
            jax.ShapeDtypeStruct((t, LANES), F32),
        ],
        scratch_shapes=[pltpu.VMEM((TOKEN_TILE + 2 * SUBLANES, 3 * width), F32)],
        compiler_params=_params("arbitrary"),
        name="gd_inproj",
    )(x, gain, wqkv, wz, wba, conv_w, alog, dtb)


def _unit_lower_inverses(lows, r, c):
    n = lows[0].shape[0]
    eye = (r == c).astype(F32)
    same = (r // INV_BASE) == (c // INV_BASE)
    powers = _each(lambda low: jnp.where(same, -low, 0.0), lows)
    invs = _each(lambda m: eye + m, powers)
    span = 2
    while span < INV_BASE:
        p16s = _each(lambda p: p.astype(BF16), powers)
        powers = _each(lambda p: _dot(p, p), p16s)
        invs = _each(lambda inv, p: inv + _dot(inv.astype(BF16), p.astype(BF16)), invs, powers)
        span *= 2
    size = INV_BASE
    while size < n:
        pair = (r // (2 * size)) == (c // (2 * size))
        take = pair & jnp.logical_not(same)
        offs = _each(lambda low: jnp.where(take, low, 0.0).astype(BF16), lows)
        i16s = _each(lambda inv: inv.astype(BF16), invs)
        mids = _each(lambda off, i16: _dot(off, i16).astype(BF16), offs, i16s)
        invs = _each(lambda inv, i16, mid: inv - _dot(i16, mid), invs, i16s, mids)
        same = pair
        size *= 2
    return invs


def _gdn_kernel(rows_ref, on_ref, q_ref, k_ref, v_ref, z_ref, o_ref):
    ck = GD_CHUNK
    n_chunks = q_ref.shape[1] // ck
    heads = range(q_ref.shape[2] // LANES)
    r = lax.broadcasted_iota(jnp.int32, (ck, ck), 0)
    c = lax.broadcasted_iota(jnp.int32, (ck, ck), 1)
    incl = r >= c
    strict = r > c

    def body(n, states):
        start = pl.multiple_of(n * ck, ck)
        rows = pl.ds(start, ck)
        lanes = [slice(hd * LANES, (hd + 1) * LANES) for hd in heads]
        qs = [q_ref[0, rows, ln] for ln in lanes]
        ks = [k_ref[0, rows, ln] for ln in lanes]
        vs = [v_ref[0, rows, ln].astype(F32) for ln in lanes]
        beta_rs = [jnp.broadcast_to(rows_ref[0, 0, hd, pl.ds(n, 1), :], (ck, ck)) for hd in heads]
        g_rs = [jnp.broadcast_to(rows_ref[0, 1, hd, pl.ds(n, 1), :], (ck, ck)) for hd in heads]
        beta_cs = _each(lambda t: t.T, beta_rs)
        g_cs = _each(lambda t: t.T, g_rs)
        g_lasts = _each(lambda t: t[:, ck - 1:ck], g_rs)

        decays = _each(lambda gc, gr: jnp.exp(jnp.where(incl, gc - gr, -jnp.inf)), g_cs, g_rs)
        kfs = _each(lambda k: k.astype(F32), ks)
        kbs = _each(lambda kf, bc: kf * bc, kfs, beta_cs)
        kks = _each(lambda kb, k: _dot_nt(kb.astype(BF16), k), kbs, ks)
        lows = _each(lambda kk, d: jnp.where(strict, kk * d, 0.0), kks, decays)
        invs = _unit_lower_inverses(lows, r, c)
        rhss = _each(lambda v, bc, kb, gc: jnp.concatenate([v * bc, kb * jnp.exp(gc)], axis=1).astype(BF16),
                     vs, beta_cs, kbs, g_cs)
        uws = _each(lambda inv, rhs: _dot(inv.astype(BF16), rhs), invs, rhss)
        qks = _each(_dot_nt, qs, ks)
        a_qks = _each(lambda qk, d: jnp.where(incl, qk * d, 0.0).astype(BF16), qks, decays)
        q_decs = _each(lambda q, gc: (q.astype(F32) * jnp.exp(gc)).astype(BF16), qs, g_cs)
        k_decs = _each(lambda kf, gl, gc: (kf * jnp.exp(gl - gc)).astype(BF16), kfs, g_lasts, g_cs)

        sbs = _each(lambda s: s.astype(BF16), states)
        wss = _each(lambda uw, sb: _dot(uw[:, LANES:].astype(BF16), sb), uws, sbs)
        qss = _each(_dot, q_decs, sbs)
        v_news = _each(lambda uw, ws: (uw[:, :LANES] - ws).astype(BF16), uws, wss)
        kvs = _each(_dot_tn, k_decs, v_news)
        avs = _each(_dot, a_qks, v_news)
        states = _each(lambda s, gl, kv: s * jnp.exp(gl) + kv, states, g_lasts, kvs)

        for ln, qsv, av in zip(lanes, qss, avs):
            o = _rms_rows(qsv + av, on_ref[...])
            zf = z_ref[0, rows, ln].astype(F32)
            o_ref[0, rows, ln] = (o * (zf * jax.nn.sigmoid(zf))).astype(o_ref.dtype)
        return tuple(states)

    lax.fori_loop(0, n_chunks, body, tuple(jnp.zeros((LANES, LANES), F32) for _ in heads))


def _gdn(qkv, z, rows, out_gain, heads):
    b, s, _ = qkv.shape
    n_chunks = s // GD_CHUNK
    grp = GD_HEAD_GROUP
    wide = grp * LANES
    ngrp = heads // grp
    return pl.pallas_call(
        _gdn_kernel,
        grid=(b, ngrp),
        in_specs=[
            pl.BlockSpec((1, 2, grp, n_chunks, GD_CHUNK), lambda bi, h: (bi, 0, h, 0, 0)),
            _resident((1, LANES)),
            pl.BlockSpec((1, s, wide), lambda bi, h: (bi, 0, h)),
            pl.BlockSpec((1, s, wide), lambda bi, h: (bi, 0, ngrp + h)),
            pl.BlockSpec((1, s, wide), lambda bi, h: (bi, 0, 2 * ngrp + h)),
            pl.BlockSpec((1, s, wide), lambda bi, h: (bi, 0, h)),
        ],
        out_specs=pl.BlockSpec((1, s, wide), lambda bi, h: (bi, 0, h)),
        out_shape=jax.ShapeDtypeStruct((b, s, heads * LANES), BF16),
        compiler_params=_params("parallel", "parallel"),
        name="gdn",
    )(rows, out_gain, qkv, qkv, qkv, z)


def _out_mlp_kernel(x_ref, o_ref, wo_ref, g_ref, w1_ref, w2_ref, y_ref):
    x1 = x_ref[...] + _dot(o_ref[...], wo_ref[...])
    xn = _rms_rows(x1, g_ref[...]).astype(BF16)
    acc = x1
    for f0 in range(0, w1_ref.shape[1], FF_TILE):
        h = jnp.maximum(_dot(xn, w1_ref[:, f0:f0 + FF_TILE]), 0.0)
        acc = acc + _dot((h * h).astype(BF16), w2_ref[f0:f0 + FF_TILE, :])
    y_ref[...] = acc


def _out_mlp(x, o, wo, gain, w1, w2):
    t, d = x.shape
    return pl.pallas_call(
        _out_mlp_kernel,
        grid=(t // TOKEN_TILE,),
        in_specs=[
            pl.BlockSpec((TOKEN_TILE, d), lambda i: (i, 0)),
            pl.BlockSpec((TOKEN_TILE, o.shape[1]), lambda i: (i, 0)),
            _resident(wo.shape),
            _resident((1, d)),
            _resident(w1.shape),
            _resident(w2.shape),
        ],
        out_specs=pl.BlockSpec((TOKEN_TILE, d), lambda i: (i, 0)),
        out_shape=jax.ShapeDtypeStruct((t, d), F32),
        compiler_params=_params("parallel"),
        name="out_mlp",
    )(x, o, wo, gain, w1, w2)


def _lane_row(vec, offset=0):
    return jnp.zeros((1, LANES), F32).at[0, offset:offset + vec.shape[0]].set(vec.astype(F32))


def kernel(x, mix_norm, mlp_norm, mlp_w_in, mlp_w_out, da_w_in, da_q_norm, da_k_norm, da_lambda_q1, da_lambda_k1, da_lambda_q2, da_lambda_k2, da_sub_norm, da_w_out, gd_w_in, gd_conv_w, gd_a_log, gd_dt_bias, gd_out_norm, gd_w_out):
    batch, seq, d = x.shape
    depth = mix_norm.shape[0]
    n_mixers = 2
    da_heads = da_w_out.shape[1] // LANES
    gd_width = gd_w_out.shape[1]
    gd_heads = gd_a_log.shape[1]
    assert seq % TOKEN_TILE == 0 and seq % ATTN_TILE == 0 and seq % GD_CHUNK == 0
    assert gd_width // gd_heads == LANES and da_w_out.shape[1] // da_heads == LANES

    xf = x.reshape(batch * seq, d)
    for i in range(depth):
        j = i // n_mixers
        gain = mix_norm[i].reshape(1, d)
        if i % n_mixers == 0:
            qk_width = da_heads * 2 * DA_HEAD_DIM
            q_gain = jnp.tile(da_q_norm[j], 2).reshape(1, LANES) * (DA_HEAD_DIM ** -0.5)
            k_gain = jnp.tile(da_k_norm[j], 2).reshape(1, LANES)
            qk, vt = _da_inproj(xf, gain, da_w_in[j].astype(BF16), q_gain, k_gain, qk_width, batch, seq)
            lam_vecs = jnp.stack([da_lambda_q1[j], da_lambda_k1[j], da_lambda_q2[j], da_lambda_k2[j]])
            lam_init = 0.8 - 0.6 * math.exp(-0.3 * i)
            o = _diff_attn(qk.reshape(batch, seq, -1), vt, lam_vecs.astype(F32),
                           da_sub_norm[j].reshape(1, LANES), lam_init)
            w_out = da_w_out[j]
        else:
            w_in = gd_w_in[j]
            wqkv = w_in[:, :3 * gd_width].astype(BF16)
            wz = w_in[:, 3 * gd_width:4 * gd_width].astype(BF16)
            wba = jnp.zeros((d, LANES), BF16).at[:, :2 * gd_heads].set(w_in[:, 4 * gd_width:].astype(BF16))
            qkv, z, gb = _gd_inproj(xf, gain, wqkv, wz, wba, gd_conv_w[j],
                                    _lane_row(gd_a_log[j], gd_heads), _lane_row(gd_dt_bias[j], gd_heads),
                                    seq, gd_heads, LANES ** -0.5)
            rows = gb[:, :2 * gd_heads].reshape(batch, seq // GD_CHUNK, GD_CHUNK, 2, gd_heads)
            rows = rows.transpose(0, 3, 4, 1, 2)
            o = _gdn(qkv.reshape(batch, seq, -1), z.reshape(batch, seq, -1), rows,
                     gd_out_norm[j].reshape(1, LANES), gd_heads)
            w_out = gd_w_out[j]
        xf = _out_mlp(xf, o.reshape(batch * seq, -1), w_out.astype(BF16), mlp_norm[i].reshape(1, d),
                      mlp_w_in[i].astype(BF16), mlp_w_out[i].astype(BF16))
    return xf.reshape(batch, seq, d)
```

```python
import functools
import math

import jax
import jax.numpy as jnp
from jax import lax
from jax.experimental import pallas as pl
from jax.experimental.pallas import tpu as pltpu

F32 = jnp.float32
BF16 = jnp.bfloat16

EPS = 1e-6
LANES = 128
SUBLANES = 8
BF16_ROWS = 16
MXU_DIM = 256
VMEM_LIMIT_BYTES = 56 * 1024 * 1024

DA_HEAD_DIM = 64
GD_CONV = 4
GD_CHUNK = 128
INV_BASE = 16
DA_HEAD_GROUP = 8
GD_HEAD_GROUP = 8

TOKEN_TILE = 512
COL_TILE = 512
ATTN_TILE = 256
FF_TILE = 1024


def _params(*sem):
    return pltpu.CompilerParams(dimension_semantics=sem, vmem_limit_bytes=VMEM_LIMIT_BYTES)


def _resident(shape):
    nd = len(shape)
    return pl.BlockSpec(shape, lambda *_: (0,) * nd, pipeline_mode=pl.Buffered(1))


def _rms_rows(x, gain):
    return x * lax.rsqrt(jnp.mean(x * x, axis=-1, keepdims=True) + EPS) * gain


def _dot(a, b):
    return jnp.dot(a, b, preferred_element_type=F32)


def _dot_nt(a, b):
    return lax.dot_general(a, b, (((1,), (1,)), ((), ())), preferred_element_type=F32)


def _dot_tn(a, b):
    return lax.dot_general(a, b, (((0,), (0,)), ((), ())), preferred_element_type=F32)


def _each(fn, *per_head):
    return [fn(*args) for args in zip(*per_head)]


def _da_inproj_kernel(x_ref, g_ref, w_ref, qg_ref, kg_ref, qk_ref, vt_ref):
    xn = _rms_rows(x_ref[...], g_ref[...]).astype(BF16)
    tm = x_ref.shape[0]
    n_qk = qk_ref.shape[1]
    qk_width = n_qk // 2
    row = lax.broadcasted_iota(jnp.int32, (MXU_DIM, MXU_DIM), 0)
    col = lax.broadcasted_iota(jnp.int32, (MXU_DIM, MXU_DIM), 1)
    same_half = ((row // DA_HEAD_DIM) == (col // DA_HEAD_DIM)).astype(BF16)
    for c0 in range(0, w_ref.shape[1], COL_TILE):
        acc = _dot(xn, w_ref[:, c0:c0 + COL_TILE])
        if c0 >= n_qk:
            for h0 in range(0, COL_TILE, LANES):
                head = (c0 - n_qk + h0) // LANES
                for t0 in range(0, tm, ATTN_TILE):
                    vt_ref[0, head, t0 // ATTN_TILE] = acc[t0:t0 + ATTN_TILE, h0:h0 + LANES].T.astype(vt_ref.dtype)
            continue
        gain = qg_ref[...] if c0 < qk_width else kg_ref[...]
        for h0 in range(0, COL_TILE, MXU_DIM):
            blk = acc[:, h0:h0 + MXU_DIM]
            ss = _dot((blk * blk).astype(BF16), same_half)
            y = blk * lax.rsqrt(ss * (1.0 / DA_HEAD_DIM) + EPS) * gain
            qk_ref[:, c0 + h0:c0 + h0 + MXU_DIM] = y.astype(qk_ref.dtype)


def _da_inproj(x, gain, w, q_gain, k_gain, qk_width, batch, seq):
    t, d = x.shape
    n = w.shape[1]
    heads = (n - 2 * qk_width) // LANES
    per_seq = seq // TOKEN_TILE
    tiles = TOKEN_TILE // ATTN_TILE
    return pl.pallas_call(
        _da_inproj_kernel,
        grid=(t // TOKEN_TILE,),
        in_specs=[
            pl.BlockSpec((TOKEN_TILE, d), lambda i: (i, 0)),
            _resident((1, d)),
            _resident((d, n)),
            _resident((1, MXU_DIM)),
            _resident((1, MXU_DIM)),
        ],
        out_specs=[
            pl.BlockSpec((TOKEN_TILE, 2 * qk_width), lambda i: (i, 0)),
            pl.BlockSpec((1, heads, tiles, LANES, ATTN_TILE), lambda i: (i // per_seq, 0, i % per_seq, 0, 0)),
        ],
        out_shape=[
            jax.ShapeDtypeStruct((t, 2 * qk_width), BF16),
            jax.ShapeDtypeStruct((batch, heads, seq // ATTN_TILE, LANES, ATTN_TILE), BF16),
        ],
        compiler_params=_params("parallel"),
        name="da_inproj",
    )(x, gain, w, q_gain, k_gain)


def _diff_attn_kernel(lam_ref, sn_ref, q_ref, k_ref, vt_ref, o_ref, m_ref, acc_ref, *, lam_init):
    tq = q_ref.shape[1]
    i = pl.program_id(2)

    lv = lam_ref[...]
    lam = (jnp.exp(jnp.sum(lv[0:1] * lv[1:2], axis=-1, keepdims=True))
           - jnp.exp(jnp.sum(lv[2:3] * lv[3:4], axis=-1, keepdims=True)) + lam_init)

    heads = range(q_ref.shape[2] // LANES)
    lanes = [slice(hd * LANES, (hd + 1) * LANES) for hd in heads]

    def stacked_queries(ln):
        qt = q_ref[0, :, ln].astype(F32).T
        feat = lax.broadcasted_iota(jnp.int32, qt.shape, 0)
        return jnp.concatenate([jnp.where(feat < DA_HEAD_DIM, qt, 0.0),
                                jnp.where(feat >= DA_HEAD_DIM, qt, 0.0)], axis=1).astype(BF16)

    qsts = [stacked_queries(ln) for ln in lanes]

    ones = jnp.ones((BF16_ROWS, tq), BF16)

    m_ref[...] = jnp.full(m_ref.shape, -jnp.inf, F32)
    acc_ref[...] = jnp.zeros(acc_ref.shape, F32)

    def block(j, diagonal):
        ms = [m_ref[hd] for hd in heads]
        rows = pl.ds(pl.multiple_of(j * tq, tq), tq)
        ss = [_dot(k_ref[0, rows, ln], qst).astype(BF16) for ln, qst in zip(lanes, qsts)]
        if diagonal:
            key = lax.broadcasted_iota(jnp.int32, ss[0].shape, 0)
            qry = lax.broadcasted_iota(jnp.int32, ss[0].shape, 1)
            keep = jnp.where(qry >= tq, qry - tq, qry) >= key
            ss = _each(lambda s: jnp.where(keep, s, -jnp.inf), ss)
        m_news = _each(lambda m, s: jnp.maximum(m, jnp.max(s, axis=0, keepdims=True).astype(F32)), ms, ss)
        alphas = _each(lambda m, mn: jnp.exp2(m - mn), ms, m_news)
        ps = _each(lambda s, mn: jnp.exp2(s - mn.astype(BF16)), ss, m_news)
        pvs = [_dot(jnp.concatenate([vt_ref[0, hd, j], ones], axis=0), p) for hd, p in zip(heads, ps)]
        for hd, m_new, alpha, pv in zip(heads, m_news, alphas, pvs):
            m_ref[hd] = m_new
            acc_ref[hd] = alpha * acc_ref[hd] + pv

    def full_block(j, carry):
        block(j, False)
        return carry

    lax.fori_loop(0, i, full_block, 0)
    block(i, True)

    for hd, ln in zip(heads, lanes):
        acc = acc_ref[hd]
        ot = acc[:LANES] / acc[LANES:LANES + 1]
        o = (ot[:, :tq] - lam * ot[:, tq:]).T
        o = _rms_rows(o, sn_ref[...]) * (1.0 - lam_init)
        o_ref[0, :, ln] = o.astype(o_ref.dtype)


def _diff_attn(qk, vt, lam_vecs, sub_gain, lam_init):
    b, s, _ = qk.shape
    heads, n_tiles = vt.shape[1], vt.shape[2]
    tq = ATTN_TILE
    grp = DA_HEAD_GROUP
    wide = grp * LANES
    ngrp = heads // grp
    return pl.pallas_call(
        functools.partial(_diff_attn_kernel, lam_init=lam_init),
        grid=(b, ngrp, s // tq),
        in_specs=[
            _resident(lam_vecs.shape),
            _resident((1, LANES)),
            pl.BlockSpec((1, tq, wide), lambda bi, h, i: (bi, i, h)),
            pl.BlockSpec((1, s, wide), lambda bi, h, i: (bi, 0, ngrp + h)),
            pl.BlockSpec((1, grp, n_tiles, LANES, tq), lambda bi, h, i: (bi, h, 0, 0, 0)),
        ],
        out_specs=pl.BlockSpec((1, tq, wide), lambda bi, h, i: (bi, i, h)),
        out_shape=jax.ShapeDtypeStruct((b, s, heads * LANES), BF16),
        scratch_shapes=[pltpu.VMEM((grp, 1, 2 * tq), F32),
                        pltpu.VMEM((grp, LANES + BF16_ROWS, 2 * tq), F32)],
        compiler_params=_params("parallel", "parallel", "arbitrary"),
        name="diff_attn",
    )(lam_vecs, sub_gain, qk, qk, vt)


def _gd_inproj_kernel(x_ref, g_ref, wqkv_ref, wz_ref, wba_ref, cw_ref, alog_ref, dtb_ref,
                      qkv_ref, z_ref, gb_ref, ext_ref, *, steps_per_seq, width, heads, q_scale):
    tm = x_ref.shape[0]
    i = pl.program_id(0)
    halo = SUBLANES

    @pl.when(i % steps_per_seq == 0)
    def _():
        ext_ref[0:halo, :] = jnp.zeros((halo, ext_ref.shape[1]), F32)

    xn = _rms_rows(x_ref[...], g_ref[...]).astype(BF16)

    for c0 in range(0, 3 * width, COL_TILE):
        cols = slice(c0, c0 + COL_TILE)
        acc = _dot(xn, wqkv_ref[:, cols])
        ext_ref[halo:halo + tm, cols] = acc
        y = acc * cw_ref[GD_CONV - 1:GD_CONV, cols]
        for back in range(1, GD_CONV):
            tap = GD_CONV - 1 - back
            y = y + ext_ref[halo - back:halo - back + tm, cols] * cw_ref[tap:tap + 1, cols]
        y = y * jax.nn.sigmoid(y)
        if c0 >= 2 * width:
            qkv_ref[:, cols] = y.astype(qkv_ref.dtype)
            continue
        post = q_scale if c0 < width else 1.0
        for h0 in range(0, COL_TILE, LANES):
            blk = y[:, h0:h0 + LANES]
            blk = blk * lax.rsqrt(jnp.sum(blk * blk, axis=-1, keepdims=True) + EPS)
            if post != 1.0:
                blk = blk * post
            qkv_ref[:, c0 + h0:c0 + h0 + LANES] = blk.astype(qkv_ref.dtype)

    ext_ref[0:halo, :] = ext_ref[tm:tm + halo, :]

    for c0 in range(0, width, COL_TILE):
        z_ref[:, c0:c0 + COL_TILE] = _dot(xn, wz_ref[:, c0:c0 + COL_TILE]).astype(z_ref.dtype)

    ba = _dot(xn, wba_ref[...])
    beta = jax.nn.sigmoid(ba)
    a = ba + dtb_ref[...]
    softplus = jnp.maximum(a, 0.0) + jnp.log1p(jnp.exp(-jnp.abs(a)))
    g = -jnp.exp(alog_ref[...]) * softplus
    lane = lax.broadcasted_iota(jnp.int32, (GD_CHUNK, LANES), 1)
    r = lax.broadcasted_iota(jnp.int32, (GD_CHUNK, GD_CHUNK), 0)
    c = lax.broadcasted_iota(jnp.int32, (GD_CHUNK, GD_CHUNK), 1)
    tril = (r >= c).astype(F32)
    for r0 in range(0, tm, GD_CHUNK):
        gc = jnp.dot(tril, g[r0:r0 + GD_CHUNK], preferred_element_type=F32,
                     precision=lax.Precision.HIGHEST)
        gb_ref[r0:r0 + GD_CHUNK, :] = jnp.where(lane < heads, beta[r0:r0 + GD_CHUNK], gc)


def _gd_inproj(x, gain, wqkv, wz, wba, conv_w, alog, dtb, seq, heads, q_scale):
    t, d = x.shape
    width = wz.shape[1]
    kern = functools.partial(_gd_inproj_kernel, steps_per_seq=seq // TOKEN_TILE, width=width,
                             heads=heads, q_scale=q_scale)
    return pl.pallas_call(
        kern,
        grid=(t // TOKEN_TILE,),
        in_specs=[
            pl.BlockSpec((TOKEN_TILE, d), lambda i: (i, 0)),
            _resident((1, d)),
            _resident(wqkv.shape),
            _resident(wz.shape),
            _resident(wba.shape),
            _resident(conv_w.shape),
            _resident((1, LANES)),
            _resident((1, LANES)),
        ],
        out_specs=[
            pl.BlockSpec((TOKEN_TILE, 3 * width), lambda i: (i, 0)),
            pl.BlockSpec((TOKEN_TILE, width), lambda i: (i, 0)),
            pl.BlockSpec((TOKEN_TILE, LANES), lambda i: (i, 0)),
        ],
        out_shape=[
            jax.ShapeDtypeStruct((t, 3 * width), BF16),
            jax.ShapeDtypeStruct((t, width), BF16),
            jax.ShapeDtypeStruct((t, LANES), F32),
        ],
        scratch_shapes=[pltpu.VMEM((TOKEN_TILE + 2 * SUBLANES, 3 * width), F32)],
        compiler_params=_params("arbitrary"),
        name="gd_inproj",
    )(x, gain, wqkv, wz, wba, conv_w, alog, dtb)


def _unit_lower_inverses(lows, r, c):
    n = lows[0].shape[0]
    eye = (r == c).astype(F32)
    same = (r // INV_BASE) == (c // INV_BASE)
    powers = _each(lambda low: jnp.where(same, -low, 0.0), lows)
    invs = _each(lambda m: eye + m, powers)
    span = 2
    while span < INV_BASE:
        p16s = _each(lambda p: p.astype(BF16), powers)
        powers = _each(lambda p: _dot(p, p), p16s)
        invs = _each(lambda inv, p: inv + _dot(inv.astype(BF16), p.astype(BF16)), invs, powers)
        span *= 2
    size = INV_BASE
    while size < n:
        pair = (r // (2 * size)) == (c // (2 * size))
        take = pair & jnp.logical_not(same)
        offs = _each(lambda low: jnp.where(take, low, 0.0).astype(BF16), lows)
        i16s = _each(lambda inv: inv.astype(BF16), invs)
        mids = _each(lambda off, i16: _dot(off, i16).astype(BF16), offs, i16s)
        invs = _each(lambda inv, i16, mid: inv - _dot(i16, mid), invs, i16s, mids)
        same = pair
        size *= 2
    return invs


def _gdn_kernel(rows_ref, on_ref, q_ref, k_ref, v_ref, z_ref, o_ref):
    ck = GD_CHUNK
    n_chunks = q_ref.shape[1] // ck
    heads = range(q_ref.shape[2] // LANES)
    r = lax.broadcasted_iota(jnp.int32, (ck, ck), 0)
    c = lax.broadcasted_iota(jnp.int32, (ck, ck), 1)
    incl = r >= c
    strict = r > c

    def body(n, states):
        start = pl.multiple_of(n * ck, ck)
        rows = pl.ds(start, ck)
        lanes = [slice(hd * LANES, (hd + 1) * LANES) for hd in heads]
        qs = [q_ref[0, rows, ln] for ln in lanes]
        ks = [k_ref[0, rows, ln] for ln in lanes]
        vs = [v_ref[0, rows, ln].astype(F32) for ln in lanes]
        beta_rs = [jnp.broadcast_to(rows_ref[0, 0, hd, pl.ds(n, 1), :], (ck, ck)) for hd in heads]
        g_rs = [jnp.broadcast_to(rows_ref[0, 1, hd, pl.ds(n, 1), :], (ck, ck)) for hd in heads]
        beta_cs = _each(lambda t: t.T, beta_rs)
        g_cs = _each(lambda t: t.T, g_rs)
        g_lasts = _each(lambda t: t[:, ck - 1:ck], g_rs)

        decays = _each(lambda gc, gr: jnp.exp(jnp.where(incl, gc - gr, -jnp.inf)), g_cs, g_rs)
        kfs = _each(lambda k: k.astype(F32), ks)
        kbs = _each(lambda kf, bc: kf * bc, kfs, beta_cs)
        kks = _each(lambda kb, k: _dot_nt(kb.astype(BF16), k), kbs, ks)
        lows = _each(lambda kk, d: jnp.where(strict, kk * d, 0.0), kks, decays)
        invs = _unit_lower_inverses(lows, r, c)
        rhss = _each(lambda v, bc, kb, gc: jnp.concatenate([v * bc, kb * jnp.exp(gc)], axis=1).astype(BF16),
                     vs, beta_cs, kbs, g_cs)
        uws = _each(lambda inv, rhs: _dot(inv.astype(BF16), rhs), invs, rhss)
        qks = _each(_dot_nt, qs, ks)
        a_qks = _each(lambda qk, d: jnp.where(incl, qk * d, 0.0).astype(BF16), qks, decays)
        q_decs = _each(lambda q, gc: (q.astype(F32) * jnp.exp(gc)).astype(BF16), qs, g_cs)
        k_decs = _each(lambda kf, gl, gc: (kf * jnp.exp(gl - gc)).astype(BF16), kfs, g_lasts, g_cs)

        sbs = _each(lambda s: s.astype(BF16), states)
        wss = _each(lambda uw, sb: _dot(uw[:, LANES:].astype(BF16), sb), uws, sbs)
        qss = _each(_dot, q_decs, sbs)
        v_news = _each(lambda uw, ws: (uw[:, :LANES] - ws).astype(BF16), uws, wss)
        kvs = _each(_dot_tn, k_decs, v_news)
        avs = _each(_dot, a_qks, v_news)
        states = _each(lambda s, gl, kv: s * jnp.exp(gl) + kv, states, g_lasts, kvs)

        for ln, qsv, av in zip(lanes, qss, avs):
            o = _rms_rows(qsv + av, on_ref[...])
            zf = z_ref[0, rows, ln].astype(F32)
            o_ref[0, rows, ln] = (o * (zf * jax.nn.sigmoid(zf))).astype(o_ref.dtype)
        return tuple(states)

    lax.fori_loop(0, n_chunks, body, tuple(jnp.zeros((LANES, LANES), F32) for _ in heads))


def _gdn(qkv, z, rows, out_gain, heads):
    b, s, _ = qkv.shape
    n_chunks = s // GD_CHUNK
    grp = GD_HEAD_GROUP
    wide = grp * LANES
    ngrp = heads // grp
    return pl.pallas_call(
        _gdn_kernel,
        grid=(b, ngrp),
        in_specs=[
            pl.BlockSpec((1, 2, grp, n_chunks, GD_CHUNK), lambda bi, h: (bi, 0, h, 0, 0)),
            _resident((1, LANES)),
            pl.BlockSpec((1, s, wide), lambda bi, h: (bi, 0, h)),
            pl.BlockSpec((1, s, wide), lambda bi, h: (bi, 0, ngrp + h)),
            pl.BlockSpec((1, s, wide), lambda bi, h: (bi, 0, 2 * ngrp + h)),
            pl.BlockSpec((1, s, wide), lambda bi, h: (bi, 0, h)),
        ],
        out_specs=pl.BlockSpec((1, s, wide), lambda bi, h: (bi, 0, h)),
        out_shape=jax.ShapeDtypeStruct((b, s, heads * LANES), BF16),
        compiler_params=_params("parallel", "parallel"),
        name="gdn",
    )(rows, out_gain, qkv, qkv, qkv, z)


def _out_mlp_kernel(x_ref, o_ref, wo_ref, g_ref, w1_ref, w2_ref, y_ref):
    x1 = x_ref[...] + _dot(o_ref[...], wo_ref[...])
    xn = _rms_rows(x1, g_ref[...]).astype(BF16)
    acc = x1
    for f0 in range(0, w1_ref.shape[1], FF_TILE):
        h = jnp.maximum(_dot(xn, w1_ref[:, f0:f0 + FF_TILE]), 0.0)
        acc = acc + _dot((h * h).astype(BF16), w2_ref[f0:f0 + FF_TILE, :])
    y_ref[...] = acc


def _out_mlp(x, o, wo, gain, w1, w2):
    t, d = x.shape
    return pl.pallas_call(
        _out_mlp_kernel,
        grid=(t // TOKEN_TILE,),
        in_specs=[
            pl.BlockSpec((TOKEN_TILE, d), lambda i: (i, 0)),
            pl.BlockSpec((TOKEN_TILE, o.shape[1]), lambda i: (i, 0)),
            _resident(wo.shape),
            _resident((1, d)),
            _resident(w1.shape),
            _resident(w2.shape),
        ],
        out_specs=pl.BlockSpec((TOKEN_TILE, d), lambda i: (i, 0)),
        out_shape=jax.ShapeDtypeStruct((t, d), F32),
        compiler_params=_params("parallel"),
        name="out_mlp",
    )(x, o, wo, gain, w1, w2)


def _lane_row(vec, offset=0):
    return jnp.zeros((1, LANES), F32).at[0, offset:offset + vec.shape[0]].set(vec.astype(F32))


def kernel(x, mix_norm, mlp_norm, mlp_w_in, mlp_w_out, da_w_in, da_q_norm, da_k_norm, da_lambda_q1, da_lambda_k1, da_lambda_q2, da_lambda_k2, da_sub_norm, da_w_out, gd_w_in, gd_conv_w, gd_a_log, gd_dt_bias, gd_out_norm, gd_w_out):
    batch, seq, d = x.shape
    depth = mix_norm.shape[0]
    n_mixers = 2
    da_heads = da_w_out.shape[1] // LANES
    gd_width = gd_w_out.shape[1]
    gd_heads = gd_a_log.shape[1]
    assert seq % TOKEN_TILE == 0 and seq % ATTN_TILE == 0 and seq % GD_CHUNK == 0
    assert gd_width // gd_heads == LANES and da_w_out.shape[1] // da_heads == LANES

    xf = x.reshape(batch * seq, d)
    for i in range(depth):
        j = i // n_mixers
        gain = mix_norm[i].reshape(1, d)
        if i % n_mixers == 0:
            qk_width = da_heads * 2 * DA_HEAD_DIM
            reps = MXU_DIM // DA_HEAD_DIM
            q_gain = jnp.tile(da_q_norm[j], reps).reshape(1, MXU_DIM) * (DA_HEAD_DIM ** -0.5 * math.log2(math.e))
            k_gain = jnp.tile(da_k_norm[j], reps).reshape(1, MXU_DIM)
            qk, vt = _da_inproj(xf, gain, da_w_in[j].astype(BF16), q_gain, k_gain, qk_width, batch, seq)
            lam_vecs = jnp.stack([da_lambda_q1[j], da_lambda_k1[j], da_lambda_q2[j], da_lambda_k2[j]])
            lam_init = 0.8 - 0.6 * math.exp(-0.3 * i)
            o = _diff_attn(qk.reshape(batch, seq, -1), vt, lam_vecs.astype(F32),
                           da_sub_norm[j].reshape(1, LANES), lam_init)
            w_out = da_w_out[j]
        else:
            w_in = gd_w_in[j]
            wqkv = w_in[:, :3 * gd_width].astype(BF16)
            wz = w_in[:, 3 * gd_width:4 * gd_width].astype(BF16)
            wba = jnp.zeros((d, LANES), BF16).at[:, :2 * gd_heads].set(w_in[:, 4 * gd_width:].astype(BF16))
            qkv, z, gb = _gd_inproj(xf, gain, wqkv, wz, wba, gd_conv_w[j],
                                    _lane_row(gd_a_log[j], gd_heads), _lane_row(gd_dt_bias[j], gd_heads),
                                    seq, gd_heads, LANES ** -0.5)
            rows = gb[:, :2 * gd_heads].reshape(batch, seq // GD_CHUNK, GD_CHUNK, 2, gd_heads)
            rows = rows.transpose(0, 3, 4, 1, 2)
            o = _gdn(qkv.reshape(batch, seq, -1), z.reshape(batch, seq, -1), rows,
                     gd_out_norm[j].reshape(1, LANES), gd_heads)
            w_out = gd_w_out[j]
        xf = _out_mlp(xf, o.reshape(batch * seq, -1), w_out.astype(BF16), mlp_norm[i].reshape(1, d),
                      mlp_w_in[i].astype(BF16), mlp_w_out[i].astype(BF16))
    return xf.reshape(batch, seq, d)
```

```python
import functools
import math

import jax
import jax.numpy as jnp
from jax import lax
from jax.experimental import pallas as pl
from jax.experimental.pallas import tpu as pltpu

F32 = jnp.float32
BF16 = jnp.bfloat16

EPS = 1e-6
LANES = 128
SUBLANES = 8
BF16_ROWS = 16
MXU_DIM = 256
VMEM_LIMIT_BYTES = 56 * 1024 * 1024

DA_HEAD_DIM = 64
GD_CONV = 4
GD_CHUNK = 128
INV_BASE = 16
DA_HEAD_GROUP = 8
GD_HEAD_GROUP = 8
GD_SEQ_BLOCK = 1024
GD_CHUNKS_PER_STEP = 4

TOKEN_TILE = 512
COL_TILE = 512
ATTN_TILE = 256
FF_TILE = 1024


def _params(*sem):
    return pltpu.CompilerParams(dimension_semantics=sem, vmem_limit_bytes=VMEM_LIMIT_BYTES)


def _resident(shape):
    nd = len(shape)
    return pl.BlockSpec(shape, lambda *_: (0,) * nd, pipeline_mode=pl.Buffered(1))


def _rms_rows(x, gain):
    return x * lax.rsqrt(jnp.mean(x * x, axis=-1, keepdims=True) + EPS) * gain


def _dot(a, b):
    return jnp.dot(a, b, preferred_element_type=F32)


def _dot_nt(a, b):
    return lax.dot_general(a, b, (((1,), (1,)), ((), ())), preferred_element_type=F32)


def _dot_tn(a, b):
    return lax.dot_general(a, b, (((0,), (0,)), ((), ())), preferred_element_type=F32)


def _each(fn, *per_head):
    return [fn(*args) for args in zip(*per_head)]


def _da_inproj_kernel(x_ref, g_ref, w_ref, qg_ref, kg_ref, qk_ref, vt_ref):
    xn = _rms_rows(x_ref[...], g_ref[...]).astype(BF16)
    tm = x_ref.shape[0]
    n_qk = qk_ref.shape[1]
    qk_width = n_qk // 2
    row = lax.broadcasted_iota(jnp.int32, (MXU_DIM, MXU_DIM), 0)
    col = lax.broadcasted_iota(jnp.int32, (MXU_DIM, MXU_DIM), 1)
    same_half = ((row // DA_HEAD_DIM) == (col // DA_HEAD_DIM)).astype(BF16)
    for c0 in range(0, w_ref.shape[1], COL_TILE):
        acc = _dot(xn, w_ref[:, c0:c0 + COL_TILE])
        if c0 >= n_qk:
            for h0 in range(0, COL_TILE, LANES):
                head = (c0 - n_qk + h0) // LANES
                for t0 in range(0, tm, ATTN_TILE):
                    vt_ref[0, head, t0 // ATTN_TILE] = acc[t0:t0 + ATTN_TILE, h0:h0 + LANES].T.astype(vt_ref.dtype)
            continue
        gain = qg_ref[...] if c0 < qk_width else kg_ref[...]
        for h0 in range(0, COL_TILE, MXU_DIM):
            blk = acc[:, h0:h0 + MXU_DIM]
            ss = _dot((blk * blk).astype(BF16), same_half)
            y = blk * lax.rsqrt(ss * (1.0 / DA_HEAD_DIM) + EPS) * gain
            qk_ref[:, c0 + h0:c0 + h0 + MXU_DIM] = y.astype(qk_ref.dtype)


def _da_inproj(x, gain, w, q_gain, k_gain, qk_width, batch, seq):
    t, d = x.shape
    n = w.shape[1]
    heads = (n - 2 * qk_width) // LANES
    per_seq = seq // TOKEN_TILE
    tiles = TOKEN_TILE // ATTN_TILE
    return pl.pallas_call(
        _da_inproj_kernel,
        grid=(t // TOKEN_TILE,),
        in_specs=[
            pl.BlockSpec((TOKEN_TILE, d), lambda i: (i, 0)),
            _resident((1, d)),
            _resident((d, n)),
            _resident((1, MXU_DIM)),
            _resident((1, MXU_DIM)),
        ],
        out_specs=[
            pl.BlockSpec((TOKEN_TILE, 2 * qk_width), lambda i: (i, 0)),
            pl.BlockSpec((1, heads, tiles, LANES, ATTN_TILE), lambda i: (i // per_seq, 0, i % per_seq, 0, 0)),
        ],
        out_shape=[
            jax.ShapeDtypeStruct((t, 2 * qk_width), BF16),
            jax.ShapeDtypeStruct((batch, heads, seq // ATTN_TILE, LANES, ATTN_TILE), BF16),
        ],
        compiler_params=_params("parallel"),
        name="da_inproj",
    )(x, gain, w, q_gain, k_gain)


def _diff_attn_kernel(lam_ref, sn_ref, q_ref, k_ref, vt_ref, o_ref, m_ref, acc_ref, *, lam_init):
    tq = q_ref.shape[1]
    i = pl.program_id(2)

    lv = lam_ref[...]
    lam = (jnp.exp(jnp.sum(lv[0:1] * lv[1:2], axis=-1, keepdims=True))
           - jnp.exp(jnp.sum(lv[2:3] * lv[3:4], axis=-1, keepdims=True)) + lam_init)

    heads = range(q_ref.shape[2] // LANES)
    lanes = [slice(hd * LANES, (hd + 1) * LANES) for hd in heads]

    def stacked_queries(ln):
        qt = q_ref[0, :, ln].astype(F32).T
        feat = lax.broadcasted_iota(jnp.int32, qt.shape, 0)
        return jnp.concatenate([jnp.where(feat < DA_HEAD_DIM, qt, 0.0),
                                jnp.where(feat >= DA_HEAD_DIM, qt, 0.0)], axis=1).astype(BF16)

    qsts = [stacked_queries(ln) for ln in lanes]

    ones = jnp.ones((BF16_ROWS, tq), BF16)

    m_ref[...] = jnp.full(m_ref.shape, -jnp.inf, F32)
    acc_ref[...] = jnp.zeros(acc_ref.shape, F32)

    def block(j, diagonal):
        ms = [m_ref[hd] for hd in heads]
        rows = pl.ds(pl.multiple_of(j * tq, tq), tq)
        ss = [_dot(k_ref[0, rows, ln], qst).astype(BF16) for ln, qst in zip(lanes, qsts)]
        if diagonal:
            key = lax.broadcasted_iota(jnp.int32, ss[0].shape, 0)
            qry = lax.broadcasted_iota(jnp.int32, ss[0].shape, 1)
            keep = jnp.where(qry >= tq, qry - tq, qry) >= key
            ss = _each(lambda s: jnp.where(keep, s, -jnp.inf), ss)
        m_news = _each(lambda m, s: jnp.maximum(m, jnp.max(s, axis=0, keepdims=True).astype(F32)), ms, ss)
        alphas = _each(lambda m, mn: jnp.exp2(m - mn), ms, m_news)
        ps = _each(lambda s, mn: jnp.exp2(s - mn.astype(BF16)), ss, m_news)
        pvs = [_dot(jnp.concatenate([vt_ref[0, hd, j], ones], axis=0), p) for hd, p in zip(heads, ps)]
        for hd, m_new, alpha, pv in zip(heads, m_news, alphas, pvs):
            m_ref[hd] = m_new
            acc_ref[hd] = alpha * acc_ref[hd] + pv

    def full_block(j, carry):
        block(j, False)
        return carry

    lax.fori_loop(0, i, full_block, 0)
    block(i, True)

    for hd, ln in zip(heads, lanes):
        acc = acc_ref[hd]
        ot = acc[:LANES] / acc[LANES:LANES + 1]
        o = (ot[:, :tq] - lam * ot[:, tq:]).T
        o = _rms_rows(o, sn_ref[...]) * (1.0 - lam_init)
        o_ref[0, :, ln] = o.astype(o_ref.dtype)


def _diff_attn(qk, vt, lam_vecs, sub_gain, lam_init):
    b, s, _ = qk.shape
    heads, n_tiles = vt.shape[1], vt.shape[2]
    tq = ATTN_TILE
    grp = DA_HEAD_GROUP
    wide = grp * LANES
    ngrp = heads // grp
    return pl.pallas_call(
        functools.partial(_diff_attn_kernel, lam_init=lam_init),
        grid=(b, ngrp, s // tq),
        in_specs=[
            _resident(lam_vecs.shape),
            _resident((1, LANES)),
            pl.BlockSpec((1, tq, wide), lambda bi, h, i: (bi, i, h)),
            pl.BlockSpec((1, s, wide), lambda bi, h, i: (bi, 0, ngrp + h)),
            pl.BlockSpec((1, grp, n_tiles, LANES, tq), lambda bi, h, i: (bi, h, 0, 0, 0)),
        ],
        out_specs=pl.BlockSpec((1, tq, wide), lambda bi, h, i: (bi, i, h)),
        out_shape=jax.ShapeDtypeStruct((b, s, heads * LANES), BF16),
        scratch_shapes=[pltpu.VMEM((grp, 1, 2 * tq), F32),
                        pltpu.VMEM((grp, LANES + BF16_ROWS, 2 * tq), F32)],
        compiler_params=_params("parallel", "parallel", "arbitrary"),
        name="diff_attn",
    )(lam_vecs, sub_gain, qk, qk, vt)


def _gd_inproj_kernel(x_ref, g_ref, wqkv_ref, wz_ref, wba_ref, cw_ref, alog_ref, dtb_ref,
                      qkv_ref, z_ref, gb_ref, ext_ref, *, steps_per_seq, width, heads, q_scale):
    tm = x_ref.shape[0]
    i = pl.program_id(0)
    halo = SUBLANES

    @pl.when(i % steps_per_seq == 0)
    def _():
        ext_ref[0:halo, :] = jnp.zeros((halo, ext_ref.shape[1]), F32)

    xn = _rms_rows(x_ref[...], g_ref[...]).astype(BF16)

    for c0 in range(0, 3 * width, COL_TILE):
        cols = slice(c0, c0 + COL_TILE)
        acc = _dot(xn, wqkv_ref[:, cols])
        ext_ref[halo:halo + tm, cols] = acc
        y = acc * cw_ref[GD_CONV - 1:GD_CONV, cols]
        for back in range(1, GD_CONV):
            tap = GD_CONV - 1 - back
            y = y + ext_ref[halo - back:halo - back + tm, cols] * cw_ref[tap:tap + 1, cols]
        y = y * jax.nn.sigmoid(y)
        if c0 >= 2 * width:
            qkv_ref[:, cols] = y.astype(qkv_ref.dtype)
            continue
        post = q_scale if c0 < width else 1.0
        for h0 in range(0, COL_TILE, LANES):
            blk = y[:, h0:h0 + LANES]
            blk = blk * lax.rsqrt(jnp.sum(blk * blk, axis=-1, keepdims=True) + EPS)
            if post != 1.0:
                blk = blk * post
            qkv_ref[:, c0 + h0:c0 + h0 + LANES] = blk.astype(qkv_ref.dtype)

    ext_ref[0:halo, :] = ext_ref[tm:tm + halo, :]

    for c0 in range(0, width, COL_TILE):
        z_ref[:, c0:c0 + COL_TILE] = _dot(xn, wz_ref[:, c0:c0 + COL_TILE]).astype(z_ref.dtype)

    ba = _dot(xn, wba_ref[...])
    beta = jax.nn.sigmoid(ba)
    a = ba + dtb_ref[...]
    softplus = jnp.maximum(a, 0.0) + jnp.log1p(jnp.exp(-jnp.abs(a)))
    g = -jnp.exp(alog_ref[...]) * softplus
    lane = lax.broadcasted_iota(jnp.int32, (GD_CHUNK, LANES), 1)
    r = lax.broadcasted_iota(jnp.int32, (GD_CHUNK, GD_CHUNK), 0)
    c = lax.broadcasted_iota(jnp.int32, (GD_CHUNK, GD_CHUNK), 1)
    tril = (r >= c).astype(F32)
    for r0 in range(0, tm, GD_CHUNK):
        gc = jnp.dot(tril, g[r0:r0 + GD_CHUNK], preferred_element_type=F32,
                     precision=lax.Precision.HIGHEST)
        gb_ref[r0:r0 + GD_CHUNK, :] = jnp.where(lane < heads, beta[r0:r0 + GD_CHUNK], gc)


def _gd_inproj(x, gain, wqkv, wz, wba, conv_w, alog, dtb, seq, heads, q_scale):
    t, d = x.shape
    width = wz.shape[1]
    kern = functools.partial(_gd_inproj_kernel, steps_per_seq=seq // TOKEN_TILE, width=width,
                             heads=heads, q_scale=q_scale)
    return pl.pallas_call(
        kern,
        grid=(t // TOKEN_TILE,),
        in_specs=[
            pl.BlockSpec((TOKEN_TILE, d), lambda i: (i, 0)),
            _resident((1, d)),
            _resident(wqkv.shape),
            _resident(wz.shape),
            _resident(wba.shape),
            _resident(conv_w.shape),
            _resident((1, LANES)),
            _resident((1, LANES)),
        ],
        out_specs=[
            pl.BlockSpec((TOKEN_TILE, 3 * width), lambda i: (i, 0)),
            pl.BlockSpec((TOKEN_TILE, width), lambda i: (i, 0)),
            pl.BlockSpec((TOKEN_TILE, LANES), lambda i: (i, 0)),
        ],
        out_shape=[
            jax.ShapeDtypeStruct((t, 3 * width), BF16),
            jax.ShapeDtypeStruct((t, width), BF16),
            jax.ShapeDtypeStruct((t, LANES), F32),
        ],
        scratch_shapes=[pltpu.VMEM((TOKEN_TILE + 2 * SUBLANES, 3 * width), F32)],
        compiler_params=_params("arbitrary"),
        name="gd_inproj",
    )(x, gain, wqkv, wz, wba, conv_w, alog, dtb)


def _fold_blocks(full, size):
    n = full.shape[0]
    return jnp.sum(full.reshape(n // size, size, n), axis=0)


def _unfold_blocks(compact, keep):
    n = compact.shape[1]
    return jnp.where(keep, jnp.tile(compact, (n // compact.shape[0], 1)), jnp.zeros((), compact.dtype))


def _unit_lower_inverses(lows, r, c):
    n = lows[0].shape[0]
    size = INV_BASE
    same = (r // size) == (c // size)
    rc = lax.broadcasted_iota(jnp.int32, (size, n), 0)
    cc = lax.broadcasted_iota(jnp.int32, (size, n), 1)
    eye = (cc % size == rc).astype(F32)
    dense = _each(lambda low: jnp.where(same, -low, 0.0), lows)
    powers = _each(lambda m: _fold_blocks(m, size), dense)
    invs = _each(lambda p: eye + p, powers)
    dense = _each(lambda m: m.astype(BF16), dense)
    span = 2
    while span < size:
        powers = _each(lambda p, d: _dot(p.astype(BF16), d), powers, dense)
        dense = _each(lambda p: _unfold_blocks(p.astype(BF16), same), powers)
        invs = _each(lambda inv, d: inv + _dot(inv.astype(BF16), d), invs, dense)
        span *= 2
    while size < n:
        pair = (r // (2 * size)) == (c // (2 * size))
        below = pair & ((r // size) == (c // size) + 1)
        left = (lax.broadcasted_iota(jnp.int32, (size, n), 1) // size) % 2 == 0
        dense = _each(lambda inv: _unfold_blocks(inv.astype(BF16), same), invs)
        offs = _each(lambda low: _fold_blocks(jnp.where(below, low, 0.0), size).astype(BF16), lows)
        mids = _each(lambda off, d: _unfold_blocks(_dot(off, d).astype(BF16), below), offs, dense)
        tails = _each(lambda inv, mid: _dot(inv.astype(BF16), mid), invs, mids)
        invs = _each(lambda inv, tail: jnp.concatenate([jnp.where(left, inv, 0.0),
                                                        jnp.where(left, -tail, inv)], axis=0), invs, tails)
        same = pair
        size *= 2
    return invs


def _gdn_kernel(rows_ref, on_ref, q_ref, k_ref, v_ref, z_ref, o_ref, state_ref):
    ck = GD_CHUNK
    n_chunks = q_ref.shape[1] // ck
    heads = range(q_ref.shape[2] // LANES)
    r = lax.broadcasted_iota(jnp.int32, (ck, ck), 0)
    c = lax.broadcasted_iota(jnp.int32, (ck, ck), 1)
    incl = r >= c
    strict = r > c

    lanes = [slice(hd * LANES, (hd + 1) * LANES) for hd in heads]

    def prepare(chunks):
        items = [(ci, hd) for ci in range(len(chunks)) for hd in heads]
        rows = [pl.ds(pl.multiple_of(n * ck, ck), ck) for n in chunks]
        qs = [q_ref[0, rows[ci], lanes[hd]] for ci, hd in items]
        ks = [k_ref[0, rows[ci], lanes[hd]] for ci, hd in items]
        vs = [v_ref[0, rows[ci], lanes[hd]].astype(F32) for ci, hd in items]
        beta_rs = [jnp.broadcast_to(rows_ref[0, 0, hd, pl.ds(chunks[ci], 1), :], (ck, ck)) for ci, hd in items]
        g_rs = [jnp.broadcast_to(rows_ref[0, 1, hd, pl.ds(chunks[ci], 1), :], (ck, ck)) for ci, hd in items]
        beta_cs = _each(lambda t: t.T, beta_rs)
        g_cs = _each(lambda t: t.T, g_rs)
        g_lasts = _each(lambda t: t[:, ck - 1:ck], g_rs)

        decays = _each(lambda gc, gr: jnp.exp(jnp.where(incl, gc - gr, -jnp.inf)), g_cs, g_rs)
        kfs = _each(lambda k: k.astype(F32), ks)
        kbs = _each(lambda kf, bc: kf * bc, kfs, beta_cs)
        kks = _each(lambda kb, k: _dot_nt(kb.astype(BF16), k), kbs, ks)
        lows = _each(lambda kk, d: jnp.where(strict, kk * d, 0.0), kks, decays)
        invs = _unit_lower_inverses(lows, r, c)
        rhss = _each(lambda v, bc, kb, gc: jnp.concatenate([v * bc, kb * jnp.exp(gc)], axis=1).astype(BF16),
                     vs, beta_cs, kbs, g_cs)
        uws = _each(lambda inv, rhs: _dot(inv.astype(BF16), rhs), invs, rhss)
        qks = _each(_dot_nt, qs, ks)
        a_qks = _each(lambda qk, d: jnp.where(incl, qk * d, 0.0).astype(BF16), qks, decays)
        q_decs = _each(lambda q, gc: (q.astype(F32) * jnp.exp(gc)).astype(BF16), qs, g_cs)
        k_decs = _each(lambda kf, gl, gc: (kf * jnp.exp(gl - gc)).astype(BF16), kfs, g_lasts, g_cs)
        per_chunk = len(heads)
        return [(rows[ci], [t[ci * per_chunk:(ci + 1) * per_chunk] for t in (uws, a_qks, q_decs, k_decs, g_lasts)])
                for ci in range(len(chunks))]

    def advance(states, prepared):
        uws, a_qks, q_decs, k_decs, g_lasts = prepared
        sbs = _each(lambda s: s.astype(BF16), states)
        wss = _each(lambda uw, sb: _dot(uw[:, LANES:].astype(BF16), sb), uws, sbs)
        qss = _each(_dot, q_decs, sbs)
        v_news = _each(lambda uw, ws: (uw[:, :LANES] - ws).astype(BF16), uws, wss)
        kvs = _each(_dot_tn, k_decs, v_news)
        avs = _each(_dot, a_qks, v_news)
        states = _each(lambda s, gl, kv: s * jnp.exp(gl) + kv, states, g_lasts, kvs)
        return states, _each(lambda qsv, av: qsv + av, qss, avs)

    def emit(rows, outs):
        for ln, o in zip(lanes, outs):
            o = _rms_rows(o, on_ref[...])
            zf = z_ref[0, rows, ln].astype(F32)
            o_ref[0, rows, ln] = (o * (zf * jax.nn.sigmoid(zf))).astype(o_ref.dtype)

    def body(step, states):
        first = step * GD_CHUNKS_PER_STEP
        states = list(states)
        for rows, prepared in prepare([first + off for off in range(GD_CHUNKS_PER_STEP)]):
            states, outs = advance(states, prepared)
            emit(rows, outs)
        return tuple(states)

    @pl.when(pl.program_id(2) == 0)
    def _():
        state_ref[...] = jnp.zeros(state_ref.shape, F32)

    states = lax.fori_loop(0, n_chunks // GD_CHUNKS_PER_STEP, body, tuple(state_ref[hd] for hd in heads))
    for hd in heads:
        state_ref[hd] = states[hd]


def _gdn(qkv, z, rows, out_gain, heads):
    b, s, _ = qkv.shape
    blk = GD_SEQ_BLOCK
    n_chunks = blk // GD_CHUNK
    grp = GD_HEAD_GROUP
    wide = grp * LANES
    ngrp = heads // grp
    return pl.pallas_call(
        _gdn_kernel,
        grid=(b, ngrp, s // blk),
        in_specs=[
            pl.BlockSpec((1, 2, grp, n_chunks, GD_CHUNK), lambda bi, h, t: (bi, 0, h, t, 0)),
            _resident((1, LANES)),
            pl.BlockSpec((1, blk, wide), lambda bi, h, t: (bi, t, h)),
            pl.BlockSpec((1, blk, wide), lambda bi, h, t: (bi, t, ngrp + h)),
            pl.BlockSpec((1, blk, wide), lambda bi, h, t: (bi, t, 2 * ngrp + h)),
            pl.BlockSpec((1, blk, wide), lambda bi, h, t: (bi, t, h)),
        ],
        out_specs=pl.BlockSpec((1, blk, wide), lambda bi, h, t: (bi, t, h)),
        out_shape=jax.ShapeDtypeStruct((b, s, heads * LANES), BF16),
        scratch_shapes=[pltpu.VMEM((grp, LANES, LANES), F32)],
        compiler_params=_params("parallel", "parallel", "arbitrary"),
        name="gdn",
    )(rows, out_gain, qkv, qkv, qkv, z)


def _out_mlp_kernel(x_ref, o_ref, wo_ref, g_ref, w1_ref, w2_ref, y_ref):
    x1 = x_ref[...] + _dot(o_ref[...], wo_ref[...])
    xn = _rms_rows(x1, g_ref[...]).astype(BF16)
    acc = x1
    for f0 in range(0, w1_ref.shape[1], FF_TILE):
        h = jnp.maximum(_dot(xn, w1_ref[:, f0:f0 + FF_TILE]), 0.0)
        acc = acc + _dot((h * h).astype(BF16), w2_ref[f0:f0 + FF_TILE, :])
    y_ref[...] = acc


def _out_mlp(x, o, wo, gain, w1, w2):
    t, d = x.shape
    return pl.pallas_call(
        _out_mlp_kernel,
        grid=(t // TOKEN_TILE,),
        in_specs=[
            pl.BlockSpec((TOKEN_TILE, d), lambda i: (i, 0)),
            pl.BlockSpec((TOKEN_TILE, o.shape[1]), lambda i: (i, 0)),
            _resident(wo.shape),
            _resident((1, d)),
            _resident(w1.shape),
            _resident(w2.shape),
        ],
        out_specs=pl.BlockSpec((TOKEN_TILE, d), lambda i: (i, 0)),
        out_shape=jax.ShapeDtypeStruct((t, d), F32),
        compiler_params=_params("parallel"),
        name="out_mlp",
    )(x, o, wo, gain, w1, w2)


def _lane_row(vec, offset=0):
    return jnp.zeros((1, LANES), F32).at[0, offset:offset + vec.shape[0]].set(vec.astype(F32))


def kernel(x, mix_norm, mlp_norm, mlp_w_in, mlp_w_out, da_w_in, da_q_norm, da_k_norm, da_lambda_q1, da_lambda_k1, da_lambda_q2, da_lambda_k2, da_sub_norm, da_w_out, gd_w_in, gd_conv_w, gd_a_log, gd_dt_bias, gd_out_norm, gd_w_out):
    batch, seq, d = x.shape
    depth = mix_norm.shape[0]
    n_mixers = 2
    da_heads = da_w_out.shape[1] // LANES
    gd_width = gd_w_out.shape[1]
    gd_heads = gd_a_log.shape[1]
    assert seq % TOKEN_TILE == 0 and seq % ATTN_TILE == 0 and seq % GD_SEQ_BLOCK == 0
    assert GD_SEQ_BLOCK % (GD_CHUNK * GD_CHUNKS_PER_STEP) == 0
    assert gd_width // gd_heads == LANES and da_w_out.shape[1] // da_heads == LANES

    xf = x.reshape(batch * seq, d)
    for i in range(depth):
        j = i // n_mixers
        gain = mix_norm[i].reshape(1, d)
        if i % n_mixers == 0:
            qk_width = da_heads * 2 * DA_HEAD_DIM
            reps = MXU_DIM // DA_HEAD_DIM
            q_gain = jnp.tile(da_q_norm[j], reps).reshape(1, MXU_DIM) * (DA_HEAD_DIM ** -0.5 * math.log2(math.e))
            k_gain = jnp.tile(da_k_norm[j], reps).reshape(1, MXU_DIM)
            qk, vt = _da_inproj(xf, gain, da_w_in[j].astype(BF16), q_gain, k_gain, qk_width, batch, seq)
            lam_vecs = jnp.stack([da_lambda_q1[j], da_lambda_k1[j], da_lambda_q2[j], da_lambda_k2[j]])
            lam_init = 0.8 - 0.6 * math.exp(-0.3 * i)
            o = _diff_attn(qk.reshape(batch, seq, -1), vt, lam_vecs.astype(F32),
                           da_sub_norm[j].reshape(1, LANES), lam_init)
            w_out = da_w_out[j]
        else:
            w_in = gd_w_in[j]
            wqkv = w_in[:, :3 * gd_width].astype(BF16)
            wz = w_in[:, 3 * gd_width:4 * gd_width].astype(BF16)
            wba = jnp.zeros((d, LANES), BF16).at[:, :2 * gd_heads].set(w_in[:, 4 * gd_width:].astype(BF16))
            qkv, z, gb = _gd_inproj(xf, gain, wqkv, wz, wba, gd_conv_w[j],
                                    _lane_row(gd_a_log[j], gd_heads), _lane_row(gd_dt_bias[j], gd_heads),
                                    seq, gd_heads, LANES ** -0.5)
            rows = gb[:, :2 * gd_heads].reshape(batch, seq // GD_CHUNK, GD_CHUNK, 2, gd_heads)
            rows = rows.transpose(0, 3, 4, 1, 2)
            o = _gdn(qkv.reshape(batch, seq, -1), z.reshape(batch, seq, -1), rows,
                     gd_out_norm[j].reshape(1, LANES), gd_heads)
            w_out = gd_w_out[j]
        xf = _out_mlp(xf, o.reshape(batch * seq, -1), w_out.astype(BF16), mlp_norm[i].reshape(1, d),
                      mlp_w_in[i].astype(BF16), mlp_w_out[i].astype(BF16))
    return xf.reshape(batch, seq, d)
```

```python
import functools
import math

import jax
import jax.numpy as jnp
from jax import lax
from jax.experimental import pallas as pl
from jax.experimental.pallas import tpu as pltpu

F32 = jnp.float32
BF16 = jnp.bfloat16

EPS = 1e-6
LANES = 128
SUBLANES = 8
BF16_ROWS = 16
MXU_DIM = 256
VMEM_LIMIT_BYTES = 56 * 1024 * 1024

DA_HEAD_DIM = 64
GD_CONV = 4
GD_CHUNK = 128
INV_BASE = 16
DA_HEAD_GROUP = 8
GD_HEAD_GROUP = 8
GD_SEQ_BLOCK = 1024
GD_CHUNKS_PER_STEP = 4

TOKEN_TILE = 512
COL_TILE = 512
ATTN_TILE = 256
FF_TILE = 1024


def _params(*sem):
    return pltpu.CompilerParams(dimension_semantics=sem, vmem_limit_bytes=VMEM_LIMIT_BYTES)


def _resident(shape):
    nd = len(shape)
    return pl.BlockSpec(shape, lambda *_: (0,) * nd, pipeline_mode=pl.Buffered(1))


def _rms_rows(x, gain):
    return x * lax.rsqrt(jnp.mean(x * x, axis=-1, keepdims=True) + EPS) * gain


def _dot(a, b):
    return jnp.dot(a, b, preferred_element_type=F32)


def _dot_nt(a, b):
    return lax.dot_general(a, b, (((1,), (1,)), ((), ())), preferred_element_type=F32)


def _dot_tn(a, b):
    return lax.dot_general(a, b, (((0,), (0,)), ((), ())), preferred_element_type=F32)


def _each(fn, *per_head):
    return [fn(*args) for args in zip(*per_head)]


def _da_inproj_kernel(x_ref, g_ref, w_ref, qg_ref, kg_ref, qk_ref, vt_ref):
    xn = _rms_rows(x_ref[...], g_ref[...]).astype(BF16)
    tm = x_ref.shape[0]
    n_qk = qk_ref.shape[1]
    qk_width = n_qk // 2
    row = lax.broadcasted_iota(jnp.int32, (MXU_DIM, MXU_DIM), 0)
    col = lax.broadcasted_iota(jnp.int32, (MXU_DIM, MXU_DIM), 1)
    same_half = ((row // DA_HEAD_DIM) == (col // DA_HEAD_DIM)).astype(BF16)
    for c0 in range(0, w_ref.shape[1], COL_TILE):
        acc = _dot(xn, w_ref[:, c0:c0 + COL_TILE])
        if c0 >= n_qk:
            for h0 in range(0, COL_TILE, LANES):
                head = (c0 - n_qk + h0) // LANES
                for t0 in range(0, tm, ATTN_TILE):
                    vt_ref[0, head, t0 // ATTN_TILE] = acc[t0:t0 + ATTN_TILE, h0:h0 + LANES].T.astype(vt_ref.dtype)
            continue
        gain = qg_ref[...] if c0 < qk_width else kg_ref[...]
        for h0 in range(0, COL_TILE, MXU_DIM):
            blk = acc[:, h0:h0 + MXU_DIM]
            ss = _dot((blk * blk).astype(BF16), same_half)
            y = blk * lax.rsqrt(ss * (1.0 / DA_HEAD_DIM) + EPS) * gain
            qk_ref[:, c0 + h0:c0 + h0 + MXU_DIM] = y.astype(qk_ref.dtype)


def _da_inproj(x, gain, w, q_gain, k_gain, qk_width, batch, seq):
    t, d = x.shape
    n = w.shape[1]
    heads = (n - 2 * qk_width) // LANES
    per_seq = seq // TOKEN_TILE
    tiles = TOKEN_TILE // ATTN_TILE
    return pl.pallas_call(
        _da_inproj_kernel,
        grid=(t // TOKEN_TILE,),
        in_specs=[
            pl.BlockSpec((TOKEN_TILE, d), lambda i: (i, 0)),
            _resident((1, d)),
            _resident((d, n)),
            _resident((1, MXU_DIM)),
            _resident((1, MXU_DIM)),
        ],
        out_specs=[
            pl.BlockSpec((TOKEN_TILE, 2 * qk_width), lambda i: (i, 0)),
            pl.BlockSpec((1, heads, tiles, LANES, ATTN_TILE), lambda i: (i // per_seq, 0, i % per_seq, 0, 0)),
        ],
        out_shape=[
            jax.ShapeDtypeStruct((t, 2 * qk_width), BF16),
            jax.ShapeDtypeStruct((batch, heads, seq // ATTN_TILE, LANES, ATTN_TILE), BF16),
        ],
        compiler_params=_params("parallel"),
        name="da_inproj",
    )(x, gain, w, q_gain, k_gain)


def _diff_attn_kernel(lam_ref, sn_ref, q_ref, k_ref, vt_ref, o_ref, m_ref, acc_ref, *, lam_init):
    tq = q_ref.shape[1]
    i = pl.program_id(2)

    lv = lam_ref[...]
    lam = (jnp.exp(jnp.sum(lv[0:1] * lv[1:2], axis=-1, keepdims=True))
           - jnp.exp(jnp.sum(lv[2:3] * lv[3:4], axis=-1, keepdims=True)) + lam_init)

    heads = range(q_ref.shape[2] // LANES)
    lanes = [slice(hd * LANES, (hd + 1) * LANES) for hd in heads]

    def stacked_queries(ln):
        qt = q_ref[0, :, ln].astype(F32).T
        feat = lax.broadcasted_iota(jnp.int32, qt.shape, 0)
        return jnp.concatenate([jnp.where(feat < DA_HEAD_DIM, qt, 0.0),
                                jnp.where(feat >= DA_HEAD_DIM, qt, 0.0)], axis=1).astype(BF16)

    qsts = [stacked_queries(ln) for ln in lanes]

    ones = jnp.ones((BF16_ROWS, tq), BF16)

    m_ref[...] = jnp.full(m_ref.shape, -jnp.inf, F32)
    acc_ref[...] = jnp.zeros(acc_ref.shape, F32)

    def block(j, diagonal):
        ms = [m_ref[hd] for hd in heads]
        rows = pl.ds(pl.multiple_of(j * tq, tq), tq)
        ss = [_dot(k_ref[0, rows, ln], qst).astype(BF16) for ln, qst in zip(lanes, qsts)]
        if diagonal:
            key = lax.broadcasted_iota(jnp.int32, ss[0].shape, 0)
            qry = lax.broadcasted_iota(jnp.int32, ss[0].shape, 1)
            keep = jnp.where(qry >= tq, qry - tq, qry) >= key
            ss = _each(lambda s: jnp.where(keep, s, -jnp.inf), ss)
        m_news = _each(lambda m, s: jnp.maximum(m, jnp.max(s, axis=0, keepdims=True).astype(F32)), ms, ss)
        alphas = _each(lambda m, mn: jnp.exp2(m - mn), ms, m_news)
        ps = _each(lambda s, mn: jnp.exp2(s - mn.astype(BF16)), ss, m_news)
        pvs = [_dot(jnp.concatenate([vt_ref[0, hd, j], ones], axis=0), p) for hd, p in zip(heads, ps)]
        for hd, m_new, alpha, pv in zip(heads, m_news, alphas, pvs):
            m_ref[hd] = m_new
            acc_ref[hd] = alpha * acc_ref[hd] + pv

    def full_block(j, carry):
        block(j, False)
        return carry

    lax.fori_loop(0, i, full_block, 0)
    block(i, True)

    for hd, ln in zip(heads, lanes):
        acc = acc_ref[hd]
        ot = acc[:LANES] * (1.0 / acc[LANES:LANES + 1])
        o = (ot[:, :tq] - lam * ot[:, tq:]).T
        o = _rms_rows(o, sn_ref[...] * (1.0 - lam_init))
        o_ref[0, :, ln] = o.astype(o_ref.dtype)


def _diff_attn(qk, vt, lam_vecs, sub_gain, lam_init):
    b, s, _ = qk.shape
    heads, n_tiles = vt.shape[1], vt.shape[2]
    tq = ATTN_TILE
    grp = DA_HEAD_GROUP
    wide = grp * LANES
    ngrp = heads // grp
    return pl.pallas_call(
        functools.partial(_diff_attn_kernel, lam_init=lam_init),
        grid=(b, ngrp, s // tq),
        in_specs=[
            _resident(lam_vecs.shape),
            _resident((1, LANES)),
            pl.BlockSpec((1, tq, wide), lambda bi, h, i: (bi, i, h)),
            pl.BlockSpec((1, s, wide), lambda bi, h, i: (bi, 0, ngrp + h)),
            pl.BlockSpec((1, grp, n_tiles, LANES, tq), lambda bi, h, i: (bi, h, 0, 0, 0)),
        ],
        out_specs=pl.BlockSpec((1, tq, wide), lambda bi, h, i: (bi, i, h)),
        out_shape=jax.ShapeDtypeStruct((b, s, heads * LANES), BF16),
        scratch_shapes=[pltpu.VMEM((grp, 1, 2 * tq), F32),
                        pltpu.VMEM((grp, LANES + BF16_ROWS, 2 * tq), F32)],
        compiler_params=_params("parallel", "parallel", "arbitrary"),
        name="diff_attn",
    )(lam_vecs, sub_gain, qk, qk, vt)


def _gd_inproj_kernel(x_ref, g_ref, wqkv_ref, wz_ref, wba_ref, cw_ref, alog_ref, dtb_ref,
                      qkv_ref, z_ref, gb_ref, ext_ref, *, steps_per_seq, width, heads, q_scale):
    tm = x_ref.shape[0]
    i = pl.program_id(0)
    halo = SUBLANES

    @pl.when(i % steps_per_seq == 0)
    def _():
        ext_ref[...] = jnp.zeros(ext_ref.shape, F32)

    xn = _rms_rows(x_ref[...], g_ref[...]).astype(BF16)
    first = lax.broadcasted_iota(jnp.int32, (halo, COL_TILE), 0)

    for c0 in range(0, 3 * width, COL_TILE):
        cols = slice(c0, c0 + COL_TILE)
        acc = _dot(xn, wqkv_ref[:, cols])
        tail = ext_ref[:, cols]
        ext_ref[:, cols] = acc[tm - halo:]
        y = acc * cw_ref[GD_CONV - 1:GD_CONV, cols]
        for back in range(1, GD_CONV):
            tap = GD_CONV - 1 - back
            rolled = pltpu.roll(acc, back, 0)
            head = jnp.where(first < back, pltpu.roll(tail, back, 0), rolled[:halo])
            shifted = jnp.concatenate([head, rolled[halo:]], axis=0)
            y = y + shifted * cw_ref[tap:tap + 1, cols]
        y = y * jax.nn.sigmoid(y)
        if c0 >= 2 * width:
            qkv_ref[:, cols] = y.astype(qkv_ref.dtype)
            continue
        post = q_scale if c0 < width else 1.0
        for h0 in range(0, COL_TILE, LANES):
            blk = y[:, h0:h0 + LANES]
            blk = blk * lax.rsqrt(jnp.sum(blk * blk, axis=-1, keepdims=True) + EPS)
            if post != 1.0:
                blk = blk * post
            qkv_ref[:, c0 + h0:c0 + h0 + LANES] = blk.astype(qkv_ref.dtype)

    for c0 in range(0, width, COL_TILE):
        z_ref[:, c0:c0 + COL_TILE] = _dot(xn, wz_ref[:, c0:c0 + COL_TILE]).astype(z_ref.dtype)

    ba = _dot(xn, wba_ref[...])
    beta = jax.nn.sigmoid(ba)
    a = ba + dtb_ref[...]
    softplus = jnp.maximum(a, 0.0) + jnp.log1p(jnp.exp(-jnp.abs(a)))
    g = -jnp.exp(alog_ref[...]) * softplus
    lane = lax.broadcasted_iota(jnp.int32, (GD_CHUNK, LANES), 1)
    r = lax.broadcasted_iota(jnp.int32, (GD_CHUNK, GD_CHUNK), 0)
    c = lax.broadcasted_iota(jnp.int32, (GD_CHUNK, GD_CHUNK), 1)
    tril = (r >= c).astype(F32)
    for r0 in range(0, tm, GD_CHUNK):
        gc = jnp.dot(tril, g[r0:r0 + GD_CHUNK], preferred_element_type=F32,
                     precision=lax.Precision.HIGHEST)
        gb_ref[r0:r0 + GD_CHUNK, :] = jnp.where(lane < heads, beta[r0:r0 + GD_CHUNK], gc)


def _gd_inproj(x, gain, wqkv, wz, wba, conv_w, alog, dtb, seq, heads, q_scale):
    t, d = x.shape
    width = wz.shape[1]
    kern = functools.partial(_gd_inproj_kernel, steps_per_seq=seq // TOKEN_TILE, width=width,
                             heads=heads, q_scale=q_scale)
    return pl.pallas_call(
        kern,
        grid=(t // TOKEN_TILE,),
        in_specs=[
            pl.BlockSpec((TOKEN_TILE, d), lambda i: (i, 0)),
            _resident((1, d)),
            _resident(wqkv.shape),
            _resident(wz.shape),
            _resident(wba.shape),
            _resident(conv_w.shape),
            _resident((1, LANES)),
            _resident((1, LANES)),
        ],
        out_specs=[
            pl.BlockSpec((TOKEN_TILE, 3 * width), lambda i: (i, 0)),
            pl.BlockSpec((TOKEN_TILE, width), lambda i: (i, 0)),
            pl.BlockSpec((TOKEN_TILE, LANES), lambda i: (i, 0)),
        ],
        out_shape=[
            jax.ShapeDtypeStruct((t, 3 * width), BF16),
            jax.ShapeDtypeStruct((t, width), BF16),
            jax.ShapeDtypeStruct((t, LANES), F32),
        ],
        scratch_shapes=[pltpu.VMEM((SUBLANES, 3 * width), F32)],
        compiler_params=_params("arbitrary"),
        name="gd_inproj",
    )(x, gain, wqkv, wz, wba, conv_w, alog, dtb)


def _fold_blocks(full, size):
    n = full.shape[0]
    return jnp.sum(full.reshape(n // size, size, n), axis=0)


def _unfold_blocks(compact, keep):
    n = compact.shape[1]
    return jnp.where(keep, jnp.tile(compact, (n // compact.shape[0], 1)), jnp.zeros((), compact.dtype))


def _unit_lower_inverses(lows, r, c):
    n = lows[0].shape[0]
    size = INV_BASE
    same = (r // size) == (c // size)
    rc = lax.broadcasted_iota(jnp.int32, (size, n), 0)
    cc = lax.broadcasted_iota(jnp.int32, (size, n), 1)
    eye = (cc % size == rc).astype(F32)
    dense = _each(lambda low: jnp.where(same, -low, 0.0), lows)
    powers = _each(lambda m: _fold_blocks(m, size), dense)
    invs = _each(lambda p: eye + p, powers)
    dense = _each(lambda m: m.astype(BF16), dense)
    span = 2
    while span < size:
        powers = _each(lambda p, d: _dot(p.astype(BF16), d), powers, dense)
        dense = _each(lambda p: _unfold_blocks(p.astype(BF16), same), powers)
        invs = _each(lambda inv, d: inv + _dot(inv.astype(BF16), d), invs, dense)
        span *= 2
    while size < n:
        pair = (r // (2 * size)) == (c // (2 * size))
        below = pair & ((r // size) == (c // size) + 1)
        left = (lax.broadcasted_iota(jnp.int32, (size, n), 1) // size) % 2 == 0
        dense = _each(lambda inv: _unfold_blocks(inv.astype(BF16), same), invs)
        offs = _each(lambda low: _fold_blocks(jnp.where(below, low, 0.0), size).astype(BF16), lows)
        mids = _each(lambda off, d: _unfold_blocks(_dot(off, d).astype(BF16), below), offs, dense)
        tails = _each(lambda inv, mid: _dot(inv.astype(BF16), mid), invs, mids)
        invs = _each(lambda inv, tail: jnp.concatenate([jnp.where(left, inv, 0.0),
                                                        jnp.where(left, -tail, inv)], axis=0), invs, tails)
        same = pair
        size *= 2
    return invs


def _gdn_kernel(rows_ref, on_ref, q_ref, k_ref, v_ref, z_ref, o_ref, state_ref):
    ck = GD_CHUNK
    n_chunks = q_ref.shape[1] // ck
    heads = range(q_ref.shape[2] // LANES)
    r = lax.broadcasted_iota(jnp.int32, (ck, ck), 0)
    c = lax.broadcasted_iota(jnp.int32, (ck, ck), 1)
    incl = r >= c
    strict = r > c

    lanes = [slice(hd * LANES, (hd + 1) * LANES) for hd in heads]

    def prepare(chunks):
        items = [(ci, hd) for ci in range(len(chunks)) for hd in heads]
        rows = [pl.ds(pl.multiple_of(n * ck, ck), ck) for n in chunks]
        qs = [q_ref[0, rows[ci], lanes[hd]] for ci, hd in items]
        ks = [k_ref[0, rows[ci], lanes[hd]] for ci, hd in items]
        vs = [v_ref[0, rows[ci], lanes[hd]].astype(F32) for ci, hd in items]
        beta_rs = [jnp.broadcast_to(rows_ref[0, 0, hd, pl.ds(chunks[ci], 1), :], (ck, ck)) for ci, hd in items]
        g_rs = [jnp.broadcast_to(rows_ref[0, 1, hd, pl.ds(chunks[ci], 1), :], (ck, ck)) for ci, hd in items]
        beta_cs = _each(lambda t: t.T, beta_rs)
        g_cs = _each(lambda t: t.T, g_rs)
        g_lasts = _each(lambda t: t[:, ck - 1:ck], g_rs)

        decays = _each(lambda gc, gr: jnp.exp(jnp.where(incl, gc - gr, -jnp.inf)), g_cs, g_rs)
        kfs = _each(lambda k: k.astype(F32), ks)
        kbs = _each(lambda kf, bc: kf * bc, kfs, beta_cs)
        kks = _each(lambda kb, k: _dot_nt(kb.astype(BF16), k), kbs, ks)
        lows = _each(lambda kk, d: jnp.where(strict, kk * d, 0.0), kks, decays)
        invs = _unit_lower_inverses(lows, r, c)
        rhss = _each(lambda v, bc, kb, gc: jnp.concatenate([v * bc, kb * jnp.exp(gc)], axis=1).astype(BF16),
                     vs, beta_cs, kbs, g_cs)
        uws = _each(lambda inv, rhs: _dot(inv.astype(BF16), rhs), invs, rhss)
        qks = _each(_dot_nt, qs, ks)
        a_qks = _each(lambda qk, d: jnp.where(incl, qk * d, 0.0).astype(BF16), qks, decays)
        q_decs = _each(lambda q, gc: (q.astype(F32) * jnp.exp(gc)).astype(BF16), qs, g_cs)
        k_decs = _each(lambda kf, gl, gc: (kf * jnp.exp(gl - gc)).astype(BF16), kfs, g_lasts, g_cs)
        per_chunk = len(heads)
        return [(rows[ci], [t[ci * per_chunk:(ci + 1) * per_chunk] for t in (uws, a_qks, q_decs, k_decs, g_lasts)])
                for ci in range(len(chunks))]

    def advance(states, prepared):
        uws, a_qks, q_decs, k_decs, g_lasts = prepared
        sbs = _each(lambda s: s.astype(BF16), states)
        wss = _each(lambda uw, sb: _dot(uw[:, LANES:].astype(BF16), sb), uws, sbs)
        qss = _each(_dot, q_decs, sbs)
        v_news = _each(lambda uw, ws: (uw[:, :LANES] - ws).astype(BF16), uws, wss)
        kvs = _each(_dot_tn, k_decs, v_news)
        avs = _each(_dot, a_qks, v_news)
        states = _each(lambda s, gl, kv: s * jnp.exp(gl) + kv, states, g_lasts, kvs)
        return states, _each(lambda qsv, av: qsv + av, qss, avs)

    def emit(rows, outs):
        for ln, o in zip(lanes, outs):
            o = _rms_rows(o, on_ref[...])
            zf = z_ref[0, rows, ln].astype(F32)
            o_ref[0, rows, ln] = (o * (zf * jax.nn.sigmoid(zf))).astype(o_ref.dtype)

    def body(step, states):
        first = step * GD_CHUNKS_PER_STEP
        states = list(states)
        for rows, prepared in prepare([first + off for off in range(GD_CHUNKS_PER_STEP)]):
            states, outs = advance(states, prepared)
            emit(rows, outs)
        return tuple(states)

    @pl.when(pl.program_id(2) == 0)
    def _():
        state_ref[...] = jnp.zeros(state_ref.shape, F32)

    states = lax.fori_loop(0, n_chunks // GD_CHUNKS_PER_STEP, body, tuple(state_ref[hd] for hd in heads))
    for hd in heads:
        state_ref[hd] = states[hd]


def _gdn(qkv, z, rows, out_gain, heads):
    b, s, _ = qkv.shape
    blk = GD_SEQ_BLOCK
    n_chunks = blk // GD_CHUNK
    grp = GD_HEAD_GROUP
    wide = grp * LANES
    ngrp = heads // grp
    return pl.pallas_call(
        _gdn_kernel,
        grid=(b, ngrp, s // blk),
        in_specs=[
            pl.BlockSpec((1, 2, grp, n_chunks, GD_CHUNK), lambda bi, h, t: (bi, 0, h, t, 0)),
            _resident((1, LANES)),
            pl.BlockSpec((1, blk, wide), lambda bi, h, t: (bi, t, h)),
            pl.BlockSpec((1, blk, wide), lambda bi, h, t: (bi, t, ngrp + h)),
            pl.BlockSpec((1, blk, wide), lambda bi, h, t: (bi, t, 2 * ngrp + h)),
            pl.BlockSpec((1, blk, wide), lambda bi, h, t: (bi, t, h)),
        ],
        out_specs=pl.BlockSpec((1, blk, wide), lambda bi, h, t: (bi, t, h)),
        out_shape=jax.ShapeDtypeStruct((b, s, heads * LANES), BF16),
        scratch_shapes=[pltpu.VMEM((grp, LANES, LANES), F32)],
        compiler_params=_params("parallel", "parallel", "arbitrary"),
        name="gdn",
    )(rows, out_gain, qkv, qkv, qkv, z)


def _out_mlp_kernel(x_ref, o_ref, wo_ref, g_ref, w1_ref, w2_ref, y_ref):
    x1 = x_ref[...] + _dot(o_ref[...], wo_ref[...])
    xn = _rms_rows(x1, g_ref[...]).astype(BF16)
    acc = x1
    for f0 in range(0, w1_ref.shape[1], FF_TILE):
        h = jnp.maximum(_dot(xn, w1_ref[:, f0:f0 + FF_TILE]), 0.0)
        acc = acc + _dot((h * h).astype(BF16), w2_ref[f0:f0 + FF_TILE, :])
    y_ref[...] = acc


def _out_mlp(x, o, wo, gain, w1, w2):
    t, d = x.shape
    return pl.pallas_call(
        _out_mlp_kernel,
        grid=(t // TOKEN_TILE,),
        in_specs=[
            pl.BlockSpec((TOKEN_TILE, d), lambda i: (i, 0)),
            pl.BlockSpec((TOKEN_TILE, o.shape[1]), lambda i: (i, 0)),
            _resident(wo.shape),
            _resident((1, d)),
            _resident(w1.shape),
            _resident(w2.shape),
        ],
        out_specs=pl.BlockSpec((TOKEN_TILE, d), lambda i: (i, 0)),
        out_shape=jax.ShapeDtypeStruct((t, d), F32),
        compiler_params=_params("parallel"),
        name="out_mlp",
    )(x, o, wo, gain, w1, w2)


def _lane_row(vec, offset=0):
    return jnp.zeros((1, LANES), F32).at[0, offset:offset + vec.shape[0]].set(vec.astype(F32))


def kernel(x, mix_norm, mlp_norm, mlp_w_in, mlp_w_out, da_w_in, da_q_norm, da_k_norm, da_lambda_q1, da_lambda_k1, da_lambda_q2, da_lambda_k2, da_sub_norm, da_w_out, gd_w_in, gd_conv_w, gd_a_log, gd_dt_bias, gd_out_norm, gd_w_out):
    batch, seq, d = x.shape
    depth = mix_norm.shape[0]
    n_mixers = 2
    da_heads = da_w_out.shape[1] // LANES
    gd_width = gd_w_out.shape[1]
    gd_heads = gd_a_log.shape[1]
    assert seq % TOKEN_TILE == 0 and seq % ATTN_TILE == 0 and seq % GD_SEQ_BLOCK == 0
    assert GD_SEQ_BLOCK % (GD_CHUNK * GD_CHUNKS_PER_STEP) == 0
    assert gd_width // gd_heads == LANES and da_w_out.shape[1] // da_heads == LANES

    xf = x.reshape(batch * seq, d)
    for i in range(depth):
        j = i // n_mixers
        gain = mix_norm[i].reshape(1, d)
        if i % n_mixers == 0:
            qk_width = da_heads * 2 * DA_HEAD_DIM
            reps = MXU_DIM // DA_HEAD_DIM
            q_gain = jnp.tile(da_q_norm[j], reps).reshape(1, MXU_DIM) * (DA_HEAD_DIM ** -0.5 * math.log2(math.e))
            k_gain = jnp.tile(da_k_norm[j], reps).reshape(1, MXU_DIM)
            qk, vt = _da_inproj(xf, gain, da_w_in[j].astype(BF16), q_gain, k_gain, qk_width, batch, seq)
            lam_vecs = jnp.stack([da_lambda_q1[j], da_lambda_k1[j], da_lambda_q2[j], da_lambda_k2[j]])
            lam_init = 0.8 - 0.6 * math.exp(-0.3 * i)
            o = _diff_attn(qk.reshape(batch, seq, -1), vt, lam_vecs.astype(F32),
                           da_sub_norm[j].reshape(1, LANES), lam_init)
            w_out = da_w_out[j]
        else:
            w_in = gd_w_in[j]
            wqkv = w_in[:, :3 * gd_width].astype(BF16)
            wz = w_in[:, 3 * gd_width:4 * gd_width].astype(BF16)
            wba = jnp.zeros((d, LANES), BF16).at[:, :2 * gd_heads].set(w_in[:, 4 * gd_width:].astype(BF16))
            qkv, z, gb = _gd_inproj(xf, gain, wqkv, wz, wba, gd_conv_w[j],
                                    _lane_row(gd_a_log[j], gd_heads), _lane_row(gd_dt_bias[j], gd_heads),
                                    seq, gd_heads, LANES ** -0.5)
            rows = gb[:, :2 * gd_heads].reshape(batch, seq // GD_CHUNK, GD_CHUNK, 2, gd_heads)
            rows = rows.transpose(0, 3, 4, 1, 2)
            o = _gdn(qkv.reshape(batch, seq, -1), z.reshape(batch, seq, -1), rows,
                     gd_out_norm[j].reshape(1, LANES), gd_heads)
            w_out = gd_w_out[j]
        xf = _out_mlp(xf, o.reshape(batch * seq, -1), w_out.astype(BF16), mlp_norm[i].reshape(1, d),
                      mlp_w_in[i].astype(BF16), mlp_w_out[i].astype(BF16))
    return xf.reshape(batch, seq, d)
```
